```python
import jax, jax.numpy as jnp
from jax import lax
import numpy as np

D_MODEL = 2048
BATCH = 2
SEQ = 8192
DEPTH = 4

N_MIXERS = 2
N_REC = (DEPTH + 1) // 2
N_ATT = DEPTH // 2
D_FF = 5632
LRU_WIDTH = D_MODEL
LRU_BLOCKS = 8
LRU_BLOCK_W = LRU_WIDTH // LRU_BLOCKS
CONV_WIDTH = 4
LRU_C = 8.0
ATT_PATTERNS = ((128, 1), (512, 4), (2048, 16))
N_GROUPS = len(ATT_PATTERNS)
ATT_HEADS = 8
ATT_HEAD_DIM = 128
ATT_WIDTH = ATT_HEADS * ATT_HEAD_DIM
MEM_TOKENS = 256
MEM_HEADS = 4
MEM_HEAD_DIM = D_MODEL // 8
MEM_WIDTH = MEM_HEADS * MEM_HEAD_DIM
REC_IN = 2 * LRU_WIDTH + MEM_WIDTH
ATT_IN = N_GROUPS * 3 * ATT_WIDTH + MEM_WIDTH
EPS = 1e-6
NEG_INF = -1e30

kernel_name = "hybrid_rglru_dilated_swa_macaron_encoder"


def rms_norm(x, g):
    xf = x.astype(jnp.float32)
    y = xf * lax.rsqrt(jnp.mean(xf * xf, axis=-1, keepdims=True) + EPS)
    return (y * g.astype(jnp.float32)).astype(x.dtype)


def swiglu(x, w_in, w_out):
    gate, up = jnp.split(x @ w_in, 2, axis=-1)
    return (jax.nn.silu(gate) * up) @ w_out


def alibi_slopes():
    n = N_GROUPS * ATT_HEADS
    s = jnp.exp2(-8.0 * (jnp.arange(n, dtype=jnp.float32) + 1.0) / n)
    return s.reshape(N_GROUPS, ATT_HEADS)


def centred_depthwise_conv(x, w, b):
    left = CONV_WIDTH // 2
    y = lax.conv_general_dilated(x, w[:, None, :].astype(x.dtype), window_strides=(1,),
                                 padding=[(left, CONV_WIDTH - 1 - left)],
                                 dimension_numbers=('NWC', 'WIO', 'NWC'),
                                 feature_group_count=x.shape[-1])
    return y + b.astype(x.dtype)


def linear_scan(a, b, reverse):
    if reverse:
        a, b = jnp.flip(a, 1), jnp.flip(b, 1)
    def combine(l, r):
        return (l[0] * r[0], r[0] * l[1] + r[1])
    _, h = lax.associative_scan(combine, (a, b), axis=1)
    return jnp.flip(h, 1) if reverse else h


def bidirectional_rg_lru(xc, gate_w, gate_b, lam):
    B, S, W = xc.shape
    xb = xc.reshape(B, S, LRU_BLOCKS, LRU_BLOCK_W)
    xf = xb.astype(jnp.float32)
    hs = []
    for direction in range(2):
        r = jax.nn.sigmoid(jnp.einsum('bsnc,ncz->bsnz', xb, gate_w[direction, 0]).astype(jnp.float32)
                           + gate_b[direction, 0].astype(jnp.float32))
        i = jax.nn.sigmoid(jnp.einsum('bsnc,ncz->bsnz', xb, gate_w[direction, 1]).astype(jnp.float32)
                           + gate_b[direction, 1].astype(jnp.float32))
        log_a = -LRU_C * r * jax.nn.softplus(-lam[direction].astype(jnp.float32)).reshape(LRU_BLOCKS, LRU_BLOCK_W)
        a = jnp.exp(log_a)
        u = jnp.sqrt(-jnp.expm1(2.0 * log_a)) * (i * xf)
        hs.append(linear_scan(a, u, reverse=(direction == 1)))
    return (hs[0] + hs[1]).reshape(B, S, W)


def dilated_band_attention(q, k, v, slope, dilation, half):
    B, S, H, hd = q.shape
    L = S // dilation
    def to_classes(t):
        return t.reshape(B, L, dilation, H, hd).transpose(0, 3, 2, 1, 4)
    qc, kc, vc = to_classes(q), to_classes(k), to_classes(v)
    C = half
    nb = -(-L // C)
    Lp = nb * C
    qb = jnp.pad(qc, [(0, 0)] * 3 + [(0, Lp - L), (0, 0)]).reshape(B, H, dilation, nb, C, hd)
    def windows(t):
        tb = jnp.pad(t, [(0, 0)] * 3 + [(C, Lp - L + C), (0, 0)]).reshape(B, H, dilation, nb + 2, C, hd)
        return jnp.concatenate([tb[:, :, :, :-2], tb[:, :, :, 1:-1], tb[:, :, :, 2:]], axis=4)
    kw, vw = windows(kc), windows(vc)
    scores = jnp.einsum('bhrnqd,bhrnkd->bhrnqk', qb, kw).astype(jnp.float32) * (hd ** -0.5)
    rel = jnp.arange(3 * C)[None, :] - C - jnp.arange(C)[:, None]
    key_pos = jnp.arange(nb)[:, None] * C - C + jnp.arange(3 * C)[None, :]
    valid = (jnp.abs(rel) <= half)[None] & ((key_pos >= 0) & (key_pos < L))[:, None, :]
    bias = -slope[:, None, None] * (dilation * jnp.abs(rel)).astype(jnp.float32)[None]
    scores = jnp.where(valid[None, None, None], scores + bias[None, :, None, None], NEG_INF)
    m = jnp.max(scores, axis=-1, keepdims=True)
    e = jnp.exp(scores - m)
    den = jnp.sum(e, axis=-1, keepdims=True)
    lse = (m + jnp.log(den))[..., 0]
    out = jnp.einsum('bhrnqk,bhrnkd->bhrnqd', (e / den).astype(v.dtype), vw)
    out = out.reshape(B, H, dilation, Lp, hd)[:, :, :, :L].transpose(0, 3, 2, 1, 4).reshape(B, S, H, hd)
    lse = lse.reshape(B, H, dilation, Lp)[:, :, :, :L].transpose(0, 3, 2, 1).reshape(B, S, H)
    return out, lse


def memory_attention(q, mem_n, w_kv, qk_gain):
    B, S = q.shape[:2]
    q = rms_norm(q, qk_gain[0])
    kv = (mem_n @ w_kv).reshape(B, mem_n.shape[1], 2, MEM_HEADS, MEM_HEAD_DIM)
    k = rms_norm(kv[:, :, 0], qk_gain[1])
    v = kv[:, :, 1]
    s = jnp.einsum('bshd,bmhd->bhsm', q, k).astype(jnp.float32) * (MEM_HEAD_DIM ** -0.5)
    p = jax.nn.softmax(s, axis=-1).astype(v.dtype)
    return jnp.einsum('bhsm,bmhd->bshd', p, v).reshape(B, S, MEM_WIDTH)


def recurrent_mixer(xn, mem_n, w_in, conv_w, conv_b, gate_w, gate_b, lam, w_out, mem_w_kv, mem_qk_gain):
    B, S, _ = xn.shape
    z = xn @ w_in
    gate_br = z[..., :LRU_WIDTH]
    rec_br = z[..., LRU_WIDTH:2 * LRU_WIDTH]
    q_mem = z[..., 2 * LRU_WIDTH:].reshape(B, S, MEM_HEADS, MEM_HEAD_DIM)
    xc = centred_depthwise_conv(rec_br, conv_w, conv_b)
    h = bidirectional_rg_lru(xc, gate_w, gate_b, lam).astype(xn.dtype)
    y_rec = h * jax.nn.gelu(gate_br)
    y_mem = memory_attention(q_mem, mem_n, mem_w_kv, mem_qk_gain)
    return jnp.concatenate([y_rec, y_mem], axis=-1) @ w_out


def attention_mixer(xn, mem_n, w_in, qk_gain, w_out, mem_w_kv, mem_qk_gain):
    B, S, _ = xn.shape
    z = xn @ w_in
    zg = z[..., :N_GROUPS * 3 * ATT_WIDTH].reshape(B, S, N_GROUPS, 3, ATT_HEADS, ATT_HEAD_DIM)
    q_mem = z[..., N_GROUPS * 3 * ATT_WIDTH:].reshape(B, S, MEM_HEADS, MEM_HEAD_DIM)
    slopes = alibi_slopes()
    outs, lses = [], []
    for g, (window, dilation) in enumerate(ATT_PATTERNS):
        q = rms_norm(zg[:, :, g, 0], qk_gain[0, g])
        k = rms_norm(zg[:, :, g, 1], qk_gain[1, g])
        o, l = dilated_band_attention(q, k, zg[:, :, g, 2], slopes[g], dilation, window // (2 * dilation))
        outs.append(o)
        lses.append(l)
    wts = jax.nn.softmax(jnp.stack(lses, axis=0), axis=0)
    att = jnp.einsum('gbsh,gbshd->bshd', wts, jnp.stack(outs, axis=0).astype(jnp.float32))
    att = att.astype(xn.dtype).reshape(B, S, ATT_WIDTH)
    y_mem = memory_attention(q_mem, mem_n, mem_w_kv, mem_qk_gain)
    return jnp.concatenate([att, y_mem], axis=-1) @ w_out


def setup_inputs(seed: int = 0) -> dict:
    key = jax.random.key(seed)
    ks = jax.random.split(key, 20)
    f32 = jnp.float32
    def nrm(k, shape, fan_in, scale=1.0):
        return jax.random.normal(k, shape, f32) * (scale * fan_in ** -0.5)
    def gain(k, shape):
        return 1.0 + 0.05 * jax.random.normal(k, shape, f32)
    a0 = jax.random.uniform(ks[13], (N_REC, 2, LRU_WIDTH), f32, minval=0.9, maxval=0.999)
    p = a0 ** (1.0 / LRU_C)
    rec_lambda = jnp.log(p) - jnp.log1p(-p)
    return {
        "x": jax.random.normal(ks[0], (BATCH, SEQ, D_MODEL), f32),
        "mem": jax.random.normal(ks[1], (BATCH, MEM_TOKENS, D_MODEL), f32),
        "ffn_norm": gain(ks[2], (DEPTH, 2, D_MODEL)),
        "ffn_w_in": nrm(ks[3], (DEPTH, 2, D_MODEL, 2 * D_FF), D_MODEL),
        "ffn_w_out": nrm(ks[4], (DEPTH, 2, D_FF, D_MODEL), D_FF, 0.5),
        "mix_norm": gain(ks[5], (DEPTH, D_MODEL)),
        "mem_norm": gain(ks[6], (DEPTH, D_MODEL)),
        "mem_w_kv": nrm(ks[7], (DEPTH, D_MODEL, 2 * MEM_WIDTH), D_MODEL),
        "mem_qk_gain": gain(ks[8], (DEPTH, 2, MEM_HEAD_DIM)),
        "rec_w_in": nrm(ks[9], (N_REC, D_MODEL, REC_IN), D_MODEL),
        "rec_conv_w": nrm(ks[10], (N_REC, CONV_WIDTH, LRU_WIDTH), CONV_WIDTH),
        "rec_conv_b": 0.01 * jax.random.normal(ks[11], (N_REC, LRU_WIDTH), f32),
        "rec_gate_w": nrm(ks[12], (N_REC, 2, 2, LRU_BLOCKS, LRU_BLOCK_W, LRU_BLOCK_W), LRU_BLOCK_W),
        "rec_gate_b": 0.01 * jax.random.normal(ks[14], (N_REC, 2, 2, LRU_BLOCKS, LRU_BLOCK_W), f32),
        "rec_lambda": rec_lambda,
        "rec_w_out": nrm(ks[15], (N_REC, LRU_WIDTH + MEM_WIDTH, D_MODEL), LRU_WIDTH + MEM_WIDTH, 0.5),
        "att_w_in": nrm(ks[16], (N_ATT, D_MODEL, ATT_IN), D_MODEL),
        "att_qk_gain": gain(ks[17], (N_ATT, 2, N_GROUPS, ATT_HEAD_DIM)),
        "att_w_out": nrm(ks[18], (N_ATT, ATT_WIDTH + MEM_WIDTH, D_MODEL), ATT_WIDTH + MEM_WIDTH, 0.5),
    }


def reference(x, mem, ffn_norm, ffn_w_in, ffn_w_out, mix_norm, mem_norm, mem_w_kv, mem_qk_gain,
              rec_w_in, rec_conv_w, rec_conv_b, rec_gate_w, rec_gate_b, rec_lambda, rec_w_out,
              att_w_in, att_qk_gain, att_w_out):
    for layer in range(DEPTH):
        x = x + 0.5 * swiglu(rms_norm(x, ffn_norm[layer, 0]), ffn_w_in[layer, 0], ffn_w_out[layer, 0])
        xn = rms_norm(x, mix_norm[layer])
        mem_n = rms_norm(mem, mem_norm[layer])
        j = layer // N_MIXERS
        if layer % N_MIXERS == 0:
            y = recurrent_mixer(xn, mem_n, rec_w_in[j], rec_conv_w[j], rec_conv_b[j], rec_gate_w[j],
                                rec_gate_b[j], rec_lambda[j], rec_w_out[j], mem_w_kv[layer], mem_qk_gain[layer])
        else:
            y = attention_mixer(xn, mem_n, att_w_in[j], att_qk_gain[j], att_w_out[j],
                                mem_w_kv[layer], mem_qk_gain[layer])
        x = x + y
        x = x + 0.5 * swiglu(rms_norm(x, ffn_norm[layer, 1]), ffn_w_in[layer, 1], ffn_w_out[layer, 1])
    return x
```

```python
import functools
import math

import jax
import jax.numpy as jnp
from jax import lax
from jax.experimental import pallas as pl
from jax.experimental.pallas import tpu as pltpu

F32 = jnp.float32
BF16 = jnp.bfloat16

EPS = 1e-6
NEG_INF = -1e30
LRU_C = 8.0
LRU_BLOCKS = 8
CONV_WIDTH = 4
CONV_LEFT = CONV_WIDTH // 2
ATT_PATTERNS = ((128, 1), (512, 4), (2048, 16))
ATT_HEADS = 8
ATT_HEAD_DIM = 128
ATT_WIDTH = ATT_HEADS * ATT_HEAD_DIM
MEM_HEADS = 4
SUBLANES = 8
VMEM_LIMIT_BYTES = 56 * 1024 * 1024


def _params(*semantics):
    return pltpu.CompilerParams(dimension_semantics=semantics, vmem_limit_bytes=VMEM_LIMIT_BYTES)


def _rms(x, g):
    return x * lax.rsqrt(jnp.mean(x * x, axis=-1, keepdims=True) + EPS) * g


def _dot(a, b):
    return jnp.dot(a, b, preferred_element_type=F32)


def _dot_nt(a, b):
    return lax.dot_general(a, b, (((1,), (1,)), ((), ())), preferred_element_type=F32)


def _ffn_kernel(x_ref, g_ref, wg_ref, wu_ref, wo_ref, o_ref, xn_ref, acc_ref):
    j = pl.program_id(1)

    @pl.when(j == 0)
    def _():
        xn_ref[...] = _rms(x_ref[...], g_ref[...]).astype(BF16)
        acc_ref[...] = jnp.zeros_like(acc_ref)

    xn = xn_ref[...]
    gate = _dot(xn, wg_ref[...])
    up = _dot(xn, wu_ref[...])
    h = (jax.nn.silu(gate) * up).astype(BF16)
    acc_ref[...] += _dot(h, wo_ref[...])

    @pl.when(j == pl.num_programs(1) - 1)
    def _():
        o_ref[...] = x_ref[...] + 0.5 * acc_ref[...]


def _ffn(x, g, w_in, w_out, layer, k, tm, tf):
    T, D = x.shape
    F = w_out.shape[2]
    nf = F // tf
    return pl.pallas_call(
        _ffn_kernel,
        grid=(T // tm, nf),
        in_specs=[
            pl.BlockSpec((tm, D), lambda i, j: (i, 0)),
            pl.BlockSpec((None, None, 1, D), lambda i, j: (layer, k, 0, 0)),
            pl.BlockSpec((None, None, D, tf), lambda i, j: (layer, k, 0, j)),
            pl.BlockSpec((None, None, D, tf), lambda i, j: (layer, k, 0, j + nf)),
            pl.BlockSpec((None, None, tf, D), lambda i, j: (layer, k, j, 0)),
        ],
        out_specs=pl.BlockSpec((tm, D), lambda i, j: (i, 0)),
        out_shape=jax.ShapeDtypeStruct((T, D), F32),
        scratch_shapes=[pltpu.VMEM((tm, D), BF16), pltpu.VMEM((tm, D), F32)],
        compiler_params=_params("parallel", "arbitrary"),
        name="ffn",
    )(x, g, w_in, w_in, w_out)


def _inproj_kernel(x_ref, g_ref, w_ref, o_ref, xn_ref):
    @pl.when(pl.program_id(1) == 0)
    def _():
        xn_ref[...] = _rms(x_ref[...], g_ref[...]).astype(BF16)

    o_ref[...] = _dot(xn_ref[...], w_ref[...])


def _inproj(x, g, w, idx, tm, tn):
    T, D = x.shape
    N = w.shape[2]
    return pl.pallas_call(
        _inproj_kernel,
        grid=(T // tm, N // tn),
        in_specs=[
            pl.BlockSpec((tm, D), lambda i, j: (i, 0)),
            pl.BlockSpec((None, 1, D), lambda i, j: (idx[0], 0, 0)),
            pl.BlockSpec((None, D, tn), lambda i, j: (idx[1], 0, j)),
        ],
        out_specs=pl.BlockSpec((tm, tn), lambda i, j: (i, j)),
        out_shape=jax.ShapeDtypeStruct((T, N), F32),
        scratch_shapes=[pltpu.VMEM((tm, D), BF16)],
        compiler_params=_params("parallel", "arbitrary"),
        name="inproj",
    )(x, g, w)


def _memkv_kernel(mem_ref, g_ref, w_ref, kg_ref, k_ref, v_ref, *, heads, hd):
    mem_n = _rms(mem_ref[...], g_ref[...]).astype(BF16)
    kv = _dot(mem_n, w_ref[...])
    for h in range(heads):
        sl = slice(h * hd, (h + 1) * hd)
        k_ref[:, sl] = _rms(kv[:, sl], kg_ref[...]).astype(BF16)
    v_ref[...] = kv[:, heads * hd:].astype(BF16)


def _memkv(mem, g, w_kv, qk_gain, layer):
    B, M, D = mem.shape
    mw = w_kv.shape[2] // 2
    hd = mw // MEM_HEADS
    kern = functools.partial(_memkv_kernel, heads=MEM_HEADS, hd=hd)
    return pl.pallas_call(
        kern,
        grid=(B,),
        in_specs=[
            pl.BlockSpec((None, M, D), lambda b: (b, 0, 0)),
            pl.BlockSpec((None, 1, D), lambda b: (layer, 0, 0)),
            pl.BlockSpec((None, D, 2 * mw), lambda b: (layer, 0, 0)),
            pl.BlockSpec((None, None, 1, hd), lambda b: (layer, 1, 0, 0)),
        ],
        out_specs=[pl.BlockSpec((None, M, mw), lambda b: (b, 0, 0))] * 2,
        out_shape=[jax.ShapeDtypeStruct((B, M, mw), BF16)] * 2,
        compiler_params=_params("parallel"),
        name="memkv",
    )(mem, g, w_kv, qk_gain)


def _memattn_kernel(q_ref, qg_ref, k_ref, v_ref, o_ref, *, heads, hd):
    scale = hd ** -0.5
    for h in range(heads):
        sl = slice(h * hd, (h + 1) * hd)
        q = _rms(q_ref[:, sl], qg_ref[...]).astype(BF16)
        s = _dot_nt(q, k_ref[:, sl]) * scale
        e = jnp.exp(s - jnp.max(s, axis=-1, keepdims=True))
        p = e / jnp.sum(e, axis=-1, keepdims=True)
        o_ref[:, sl] = _dot(p.astype(BF16), v_ref[:, sl]).astype(o_ref.dtype)


def _memattn(z, q_col, mk, mv, qk_gain, layer, S, tm):
    T = z.shape[0]
    _, M, mw = mk.shape
    hd = mw // MEM_HEADS
    qb = q_col // mw
    assert qb * mw == q_col
    per_b = S // tm
    kern = functools.partial(_memattn_kernel, heads=MEM_HEADS, hd=hd)
    return pl.pallas_call(
        kern,
        grid=(T // tm,),
        in_specs=[
            pl.BlockSpec((tm, mw), lambda i: (i, qb)),
            pl.BlockSpec((None, None, 1, hd), lambda i: (layer, 0, 0, 0)),
            pl.BlockSpec((None, M, mw), lambda i: (i // per_b, 0, 0)),
            pl.BlockSpec((None, M, mw), lambda i: (i // per_b, 0, 0)),
        ],
        out_specs=pl.BlockSpec((tm, mw), lambda i: (i, 0)),
        out_shape=jax.ShapeDtypeStruct((T, mw), BF16),
        compiler_params=_params("parallel"),
        name="memattn",
    )(z, qk_gain, mk, mv)


def _outproj_kernel(ya_ref, ym_ref, wa_ref, wm_ref, x_ref, o_ref):
    o_ref[...] = x_ref[...] + _dot(ya_ref[...], wa_ref[...]) + _dot(ym_ref[...], wm_ref[...])


def _outproj(x, ya, ym, w, idx, tm, tn):
    T, D = x.shape
    ka, km = ya.shape[1], ym.shape[1]
    mb = ka // km
    assert mb * km == ka
    return pl.pallas_call(
        _outproj_kernel,
        grid=(T // tm, D // tn),
        in_specs=[
            pl.BlockSpec((tm, ka), lambda i, j: (i, 0)),
            pl.BlockSpec((tm, km), lambda i, j: (i, 0)),
            pl.BlockSpec((None, ka, tn), lambda i, j: (idx, 0, j)),
            pl.BlockSpec((None, km, tn), lambda i, j: (idx, mb, j)),
            pl.BlockSpec((tm, tn), lambda i, j: (i, j)),
        ],
        out_specs=pl.BlockSpec((tm, tn), lambda i, j: (i, j)),
        out_shape=jax.ShapeDtypeStruct((T, D), F32),
        compiler_params=_params("parallel", "arbitrary"),
        name="outproj",
    )(ya, ym, w, w, x)


def _rec_kernel(*refs, reverse, final, ts, bw):
    if final:
        (prev_ref, main_ref, next_ref, cw_ref, cb_ref, gw_ref, gb_ref, lam_ref,
         gate_ref, hb_ref, o_ref, xe_ref, a_ref, u_ref, h_ref) = refs
    else:
        (prev_ref, main_ref, next_ref, cw_ref, cb_ref, gw_ref, gb_ref, lam_ref,
         o_ref, xe_ref, a_ref, u_ref, h_ref) = refs
    c = pl.program_id(1)
    nc = pl.num_programs(1)
    chunk = nc - 1 - c if reverse else c

    @pl.when(c == 0)
    def _():
        h_ref[...] = jnp.zeros_like(h_ref)

    xe_ref[0:SUBLANES, :] = jnp.where(chunk > 0, prev_ref[...], 0.0)
    xe_ref[SUBLANES:SUBLANES + ts, :] = main_ref[...]
    xe_ref[SUBLANES + ts:2 * SUBLANES + ts, :] = jnp.where(chunk < nc - 1, next_ref[...], 0.0)
    xc = cb_ref[...]
    for k in range(CONV_WIDTH):
        xc = xc + cw_ref[k:k + 1, :] * xe_ref[pl.ds(SUBLANES - CONV_LEFT + k, ts), :]

    softplus_neg_lam = jax.nn.softplus(-lam_ref[...])
    xcb = xc.astype(BF16)
    for n in range(LRU_BLOCKS):
        sl = slice(n * bw, (n + 1) * bw)
        r = jax.nn.sigmoid(_dot(xcb[:, sl], gw_ref[0, n]) + gb_ref[0:1, sl])
        i = jax.nn.sigmoid(_dot(xcb[:, sl], gw_ref[1, n]) + gb_ref[1:2, sl])
        log_a = -LRU_C * r * softplus_neg_lam[:, sl]
        a = jnp.exp(log_a)
        a_ref[:, sl] = a
        u_ref[:, sl] = jnp.sqrt(1.0 - a * a) * (i * xc[:, sl])

    def step(t, h):
        row = ts - 1 - t if reverse else t
        h = a_ref[pl.ds(row, 1), :] * h + u_ref[pl.ds(row, 1), :]
        u_ref[pl.ds(row, 1), :] = h
        return h

    h_ref[...] = lax.fori_loop(0, ts, step, h_ref[...], unroll=8)

    if final:
        h_both = u_ref[...] + hb_ref[...]
        o_ref[...] = (h_both * jax.nn.gelu(gate_ref[...])).astype(o_ref.dtype)
    else:
        o_ref[...] = u_ref[...]


def _rec_scan(z, conv_w, conv_b, gate_w, gate_b, lam, j, direction, h_other, S, ts):
    T = z.shape[0]
    W = conv_w.shape[2]
    bw = W // LRU_BLOCKS
    nc = S // ts
    sub = ts // SUBLANES
    reverse = direction == 1
    final = h_other is not None

    def chunk_of(c):
        return nc - 1 - c if reverse else c

    def main_map(b, c):
        return (b * nc + chunk_of(c), 1)

    def prev_map(b, c):
        return (jnp.maximum((b * nc + chunk_of(c)) * sub - 1, 0), 1)

    def next_map(b, c):
        return (jnp.minimum((b * nc + chunk_of(c) + 1) * sub, T // SUBLANES - 1), 1)

    in_specs = [
        pl.BlockSpec((SUBLANES, W), prev_map),
        pl.BlockSpec((ts, W), main_map),
        pl.BlockSpec((SUBLANES, W), next_map),
        pl.BlockSpec((None, CONV_WIDTH, W), lambda b, c: (j, 0, 0)),
        pl.BlockSpec((None, 1, W), lambda b, c: (j, 0, 0)),
        pl.BlockSpec((None, None, 2, LRU_BLOCKS, bw, bw), lambda b, c: (j, direction, 0, 0, 0, 0)),
        pl.BlockSpec((None, None, 2, W), lambda b, c: (j, direction, 0, 0)),
        pl.BlockSpec((None, None, 1, W), lambda b, c: (j, direction, 0, 0)),
    ]
    args = [z, z, z, conv_w, conv_b, gate_w, gate_b, lam]
    if final:
        in_specs += [
            pl.BlockSpec((ts, W), lambda b, c: (b * nc + chunk_of(c), 0)),
            pl.BlockSpec((ts, W), lambda b, c: (b * nc + chunk_of(c), 0)),
        ]
        args += [z, h_other]
    kern = functools.partial(_rec_kernel, reverse=reverse, final=final, ts=ts, bw=bw)
    return pl.pallas_call(
        kern,
        grid=(T // S, nc),
        in_specs=in_specs,
        out_specs=pl.BlockSpec((ts, W), lambda b, c: (b * nc + chunk_of(c), 0)),
        out_shape=jax.ShapeDtypeStruct((T, W), BF16 if final else F32),
        scratch_shapes=[
            pltpu.VMEM((ts + 2 * SUBLANES, W), F32),
            pltpu.VMEM((ts, W), F32),
            pltpu.VMEM((ts, W), F32),
            pltpu.VMEM((1, W), F32),
        ],
        compiler_params=_params("parallel", "arbitrary"),
        name="rec_fwd" if final else "rec_bwd",
    )(*args)


def _alibi_slopes():
    n = len(ATT_PATTERNS) * ATT_HEADS
    return [[2.0 ** (-8.0 * (g * ATT_HEADS + h + 1.0) / n) for h in range(ATT_HEADS)]
            for g in range(len(ATT_PATTERNS))]


def _attn_kernel(*refs, dilation, half, tq, length, slopes, first, last):
    (q_ref, kp_ref, kc_ref, kn_ref, vp_ref, vc_ref, vn_ref, gq_ref, gk_ref) = refs[:9]
    if first:
        o_ref, l_ref = refs[9:]
    elif last:
        po_ref, pl_ref, o_ref = refs[9:]
    else:
        po_ref, pl_ref, o_ref, l_ref = refs[9:]
    hd = ATT_HEAD_DIM
    nk = tq + 2 * half
    base = pl.program_id(2) * tq
    row = lax.broadcasted_iota(jnp.int32, (tq, nk), 0)
    col = lax.broadcasted_iota(jnp.int32, (tq, nk), 1)
    rel = col - half - row
    key = base - half + col
    valid = (jnp.abs(rel) <= half) & (key >= 0) & (key < length)
    dist = (dilation * jnp.abs(rel)).astype(F32)
    kwin = jnp.concatenate([kp_ref[...], kc_ref[...], kn_ref[...]], axis=0)
    vwin = jnp.concatenate([vp_ref[...], vc_ref[...], vn_ref[...]], axis=0).astype(BF16)
    for h in range(ATT_HEADS):
        sl = slice(h * hd, (h + 1) * hd)
        q = _rms(q_ref[:, sl], gq_ref[...]).astype(BF16)
        k = _rms(kwin[:, sl], gk_ref[...]).astype(BF16)
        s = _dot_nt(q, k) * (hd ** -0.5)
        s = jnp.where(valid, s - slopes[h] * dist, NEG_INF)
        m = jnp.max(s, axis=-1, keepdims=True)
        e = jnp.exp(s - m)
        den = jnp.sum(e, axis=-1, keepdims=True)
        o = _dot((e / den).astype(BF16), vwin[:, sl])
        lse = jnp.broadcast_to(m + jnp.log(den), (tq, hd))
        if not first:
            lse_prev = pl_ref[:, sl]
            top = jnp.maximum(lse_prev, lse)
            w_prev = jnp.exp(lse_prev - top)
            w_cur = jnp.exp(lse - top)
            tot = w_prev + w_cur
            o = (w_prev * po_ref[:, sl] + w_cur * o) / tot
            lse = top + jnp.log(tot)
        o_ref[:, sl] = o.astype(o_ref.dtype)
        if not last:
            l_ref[:, sl] = lse


def _attn_group(z, qk_gain, j, g, prev, B, S, tq, last):
    window, d = ATT_PATTERNS[g]
    half = window // (2 * d)
    T, zw = z.shape
    L = S // d
    aw = ATT_WIDTH
    zb = zw // aw
    assert zb * aw == zw and tq % half == 0 and L % tq == 0
    ratio = tq // half
    first = prev is None
    zv = z.reshape(B, L, d * zw)
    cq, ck, cv = (3 * g + p for p in range(3))

    def cur(cb):
        return pl.BlockSpec((None, tq, aw), lambda b, r, i: (b, i, r * zb + cb))

    def before(cb):
        return pl.BlockSpec((None, half, aw),
                            lambda b, r, i: (b, jnp.maximum(i * ratio - 1, 0), r * zb + cb))

    def after(cb):
        return pl.BlockSpec((None, half, aw),
                            lambda b, r, i: (b, jnp.minimum((i + 1) * ratio, L // half - 1), r * zb + cb))

    def gain(p):
        return pl.BlockSpec((None, None, None, 1, ATT_HEAD_DIM), lambda b, r, i: (j, p, g, 0, 0))

    seq_spec = pl.BlockSpec((None, tq, aw), lambda b, r, i: (b, i, r))
    in_specs = [cur(cq), before(ck), cur(ck), after(ck), before(cv), cur(cv), after(cv), gain(0), gain(1)]
    args = [zv] * 7 + [qk_gain, qk_gain]
    if not first:
        in_specs += [seq_spec, seq_spec]
        args += [prev[0].reshape(B, L, d * aw), prev[1].reshape(B, L, d * aw)]
    out_specs = [seq_spec] if last else [seq_spec, seq_spec]
    out_shape = [jax.ShapeDtypeStruct((B, L, d * aw), BF16 if last else F32)]
    if not last:
        out_shape.append(jax.ShapeDtypeStruct((B, L, d * aw), F32))
    kern = functools.partial(_attn_kernel, dilation=d, half=half, tq=tq, length=L,
                             slopes=_alibi_slopes()[g], first=first, last=last)
    outs = pl.pallas_call(
        kern,
        grid=(B, d, L // tq),
        in_specs=in_specs,
        out_specs=out_specs,
        out_shape=out_shape,
        compiler_params=_params("parallel", "parallel", "arbitrary"),
        name=f"attn_g{g}",
    )(*args)
    return [o.reshape(T, aw) for o in outs]


def kernel(x, mem, ffn_norm, ffn_w_in, ffn_w_out, mix_norm, mem_norm, mem_w_kv, mem_qk_gain,
           rec_w_in, rec_conv_w, rec_conv_b, rec_gate_w, rec_gate_b, rec_lambda, rec_w_out,
           att_w_in, att_qk_gain, att_w_out):
    B, S, D = x.shape
    T = B * S
    depth = ffn_norm.shape[0]
    W = rec_conv_w.shape[2]
    mw = mem_w_kv.shape[2] // 2

    tm_ffn = min(512, S)
    tf = min(512, ffn_w_out.shape[2])
    tm_proj = min(1024, S)
    tn_proj = min(512, D)
    ts_rec = min(256, S)
    tq = 2 * (ATT_PATTERNS[0][0] // 2)

    ffn_w_in, ffn_w_out, mem_w_kv, rec_w_in, rec_gate_w, rec_w_out, att_w_in, att_w_out = (
        w.astype(BF16) for w in (ffn_w_in, ffn_w_out, mem_w_kv, rec_w_in, rec_gate_w, rec_w_out,
                                 att_w_in, att_w_out))
    ffn_norm = ffn_norm[:, :, None, :]
    mix_norm = mix_norm[:, None, :]
    mem_norm = mem_norm[:, None, :]
    mem_qk_gain = mem_qk_gain[:, :, None, :]
    rec_conv_b = rec_conv_b[:, None, :]
    rec_gate_b = rec_gate_b.reshape(rec_gate_b.shape[0], 2, 2, W)
    rec_lambda = rec_lambda[:, :, None, :]
    att_qk_gain = att_qk_gain[:, :, :, None, :]

    x = x.reshape(T, D)
    for layer in range(depth):
        x = _ffn(x, ffn_norm, ffn_w_in, ffn_w_out, layer, 0, tm_ffn, tf)
        mk, mv = _memkv(mem, mem_norm, mem_w_kv, mem_qk_gain, layer)
        j = layer // 2
        if layer % 2 == 0:
            z = _inproj(x, mix_norm, rec_w_in, (layer, j), tm_proj, tn_proj)
            hb = _rec_scan(z, rec_conv_w, rec_conv_b, rec_gate_w, rec_gate_b, rec_lambda, j, 1, None, S, ts_rec)
            ya = _rec_scan(z, rec_conv_w, rec_conv_b, rec_gate_w, rec_gate_b, rec_lambda, j, 0, hb, S, ts_rec)
            ym = _memattn(z, 2 * W, mk, mv, mem_qk_gain, layer, S, tm_proj)
            x = _outproj(x, ya, ym, rec_w_out, j, tm_proj, tn_proj)
        else:
            z = _inproj(x, mix_norm, att_w_in, (layer, j), tm_proj, tn_proj)
            prev = None
            for g in range(len(ATT_PATTERNS)):
                prev = _attn_group(z, att_qk_gain, j, g, prev, B, S, tq, g == len(ATT_PATTERNS) - 1)
            ym = _memattn(z, len(ATT_PATTERNS) * 3 * ATT_WIDTH, mk, mv, mem_qk_gain, layer, S, tm_proj)
            x = _outproj(x, prev[0], ym, att_w_out, j, tm_proj, tn_proj)
        x = _ffn(x, ffn_norm, ffn_w_in, ffn_w_out, layer, 1, tm_ffn, tf)
    return x.reshape(B, S, D)
```

```python
import functools

import jax
import jax.numpy as jnp
import numpy as np
from jax import lax
from jax.experimental import pallas as pl
from jax.experimental.pallas import tpu as pltpu

F32 = jnp.float32
BF16 = jnp.bfloat16

EPS = 1e-6
NEG_INF = -1e30
LRU_C = 8.0
LRU_BLOCKS = 8
CONV_WIDTH = 4
CONV_LEFT = CONV_WIDTH // 2
ATT_PATTERNS = ((128, 1), (512, 4), (2048, 16))
ATT_GROUPS = len(ATT_PATTERNS)
ATT_HEADS = 8
ATT_HEAD_DIM = 128
ATT_WIDTH = ATT_HEADS * ATT_HEAD_DIM
MEM_HEADS = 4
SUBLANES = 8
LANES = 128
VMEM_LIMIT_BYTES = 56 * 1024 * 1024


def _params(*semantics):
    return pltpu.CompilerParams(dimension_semantics=semantics, vmem_limit_bytes=VMEM_LIMIT_BYTES)


def _rms(x, g):
    return x * lax.rsqrt(jnp.mean(x * x, axis=-1, keepdims=True) + EPS) * g


def _dot(a, b):
    return jnp.dot(a, b, preferred_element_type=F32)


def _dot_nt(a, b):
    return lax.dot_general(a, b, (((1,), (1,)), ((), ())), preferred_element_type=F32)


def _ffn_kernel(x_ref, g_ref, wg_ref, wu_ref, wo_ref, o_ref, xn_ref, acc_ref):
    j = pl.program_id(1)

    @pl.when(j == 0)
    def _():
        xn_ref[...] = _rms(x_ref[...], g_ref[...]).astype(BF16)
        acc_ref[...] = jnp.zeros_like(acc_ref)

    xn = xn_ref[...]
    gate = _dot(xn, wg_ref[...])
    up = _dot(xn, wu_ref[...])
    h = (jax.nn.silu(gate) * up).astype(BF16)
    acc_ref[...] += _dot(h, wo_ref[...])

    @pl.when(j == pl.num_programs(1) - 1)
    def _():
        o_ref[...] = x_ref[...] + 0.5 * acc_ref[...]


def _ffn(x, g, w_in, w_out, layer, k, tm, tf):
    T, D = x.shape
    F = w_out.shape[2]
    nf = F // tf
    return pl.pallas_call(
        _ffn_kernel,
        grid=(T // tm, nf),
        in_specs=[
            pl.BlockSpec((tm, D), lambda i, j: (i, 0)),
            pl.BlockSpec((None, None, 1, D), lambda i, j: (layer, k, 0, 0)),
            pl.BlockSpec((None, None, D, tf), lambda i, j: (layer, k, 0, j)),
            pl.BlockSpec((None, None, D, tf), lambda i, j: (layer, k, 0, j + nf)),
            pl.BlockSpec((None, None, tf, D), lambda i, j: (layer, k, j, 0)),
        ],
        out_specs=pl.BlockSpec((tm, D), lambda i, j: (i, 0)),
        out_shape=jax.ShapeDtypeStruct((T, D), F32),
        scratch_shapes=[pltpu.VMEM((tm, D), BF16), pltpu.VMEM((tm, D), F32)],
        compiler_params=_params("parallel", "arbitrary"),
        name="ffn",
    )(x, g, w_in, w_in, w_out)


def _inproj_kernel(x_ref, g_ref, w_ref, o_ref, xn_ref):
    @pl.when(pl.program_id(1) == 0)
    def _():
        xn_ref[...] = _rms(x_ref[...], g_ref[...]).astype(BF16)

    o_ref[...] = _dot(xn_ref[...], w_ref[...])


def _inproj(x, g, w, idx, tm, tn):
    T, D = x.shape
    N = w.shape[2]
    return pl.pallas_call(
        _inproj_kernel,
        grid=(T // tm, N // tn),
        in_specs=[
            pl.BlockSpec((tm, D), lambda i, j: (i, 0)),
            pl.BlockSpec((None, 1, D), lambda i, j: (idx[0], 0, 0)),
            pl.BlockSpec((None, D, tn), lambda i, j: (idx[1], 0, j)),
        ],
        out_specs=pl.BlockSpec((tm, tn), lambda i, j: (i, j)),
        out_shape=jax.ShapeDtypeStruct((T, N), F32),
        scratch_shapes=[pltpu.VMEM((tm, D), BF16)],
        compiler_params=_params("parallel", "arbitrary"),
        name="inproj",
    )(x, g, w)


def _inproj_heads_kernel(x_ref, g_ref, w_ref, qkg_ref, o_ref, xn_ref, *, tiles_per_part):
    j = pl.program_id(1)

    @pl.when(j == 0)
    def _():
        xn_ref[...] = _rms(x_ref[...], g_ref[...]).astype(BF16)

    z = _dot(xn_ref[...], w_ref[...])
    part = j // tiles_per_part
    is_qk = (part < 3 * ATT_GROUPS) & (part % 3 < 2)
    heads = o_ref.shape[0]

    @pl.when(is_qk)
    def _():
        for h in range(heads):
            o_ref[h] = _rms(z[:, h * LANES:(h + 1) * LANES], qkg_ref[...])

    @pl.when(jnp.logical_not(is_qk))
    def _():
        for h in range(heads):
            o_ref[h] = z[:, h * LANES:(h + 1) * LANES]


def _inproj_heads(x, g, w, qk_gain, idx, tm, tn):
    T, D = x.shape
    N = w.shape[2]
    assert ATT_HEAD_DIM == LANES and tn % LANES == 0 and ATT_WIDTH % tn == 0
    tpp = ATT_WIDTH // tn

    def gain_map(i, j):
        part = j // tpp
        return (idx[1], jnp.minimum(part % 3, 1), jnp.minimum(part // 3, ATT_GROUPS - 1), 0, 0)

    kern = functools.partial(_inproj_heads_kernel, tiles_per_part=tpp)
    return pl.pallas_call(
        kern,
        grid=(T // tm, N // tn),
        in_specs=[
            pl.BlockSpec((tm, D), lambda i, j: (i, 0)),
            pl.BlockSpec((None, 1, D), lambda i, j: (idx[0], 0, 0)),
            pl.BlockSpec((None, D, tn), lambda i, j: (idx[1], 0, j)),
            pl.BlockSpec((None, None, None, 1, LANES), gain_map),
        ],
        out_specs=pl.BlockSpec((tn // LANES, tm, LANES), lambda i, j: (j, i, 0)),
        out_shape=jax.ShapeDtypeStruct((N // LANES, T, LANES), F32),
        scratch_shapes=[pltpu.VMEM((tm, D), BF16)],
        compiler_params=_params("parallel", "arbitrary"),
        name="inproj_heads",
    )(x, g, w, qk_gain)


def _memkv_kernel(mem_ref, g_ref, w_ref, kg_ref, k_ref, v_ref, *, heads, hd):
    mem_n = _rms(mem_ref[...], g_ref[...]).astype(BF16)
    kv = _dot(mem_n, w_ref[...])
    for h in range(heads):
        sl = slice(h * hd, (h + 1) * hd)
        k_ref[:, sl] = _rms(kv[:, sl], kg_ref[...]).astype(BF16)
    v_ref[...] = kv[:, heads * hd:].astype(BF16)


def _memkv(mem, g, w_kv, qk_gain, layer):
    B, M, D = mem.shape
    mw = w_kv.shape[2] // 2
    hd = mw // MEM_HEADS
    kern = functools.partial(_memkv_kernel, heads=MEM_HEADS, hd=hd)
    return pl.pallas_call(
        kern,
        grid=(B,),
        in_specs=[
            pl.BlockSpec((None, M, D), lambda b: (b, 0, 0)),
            pl.BlockSpec((None, 1, D), lambda b: (layer, 0, 0)),
            pl.BlockSpec((None, D, 2 * mw), lambda b: (layer, 0, 0)),
            pl.BlockSpec((None, None, 1, hd), lambda b: (layer, 1, 0, 0)),
        ],
        out_specs=[pl.BlockSpec((None, M, mw), lambda b: (b, 0, 0))] * 2,
        out_shape=[jax.ShapeDtypeStruct((B, M, mw), BF16)] * 2,
        compiler_params=_params("parallel"),
        name="memkv",
    )(mem, g, w_kv, qk_gain)


def _memattn_kernel(q_ref, qg_ref, k_ref, v_ref, o_ref, *, heads, hd, head_major):
    scale = hd ** -0.5
    per = hd // LANES
    for h in range(heads):
        sl = slice(h * hd, (h + 1) * hd)
        if head_major:
            qh = jnp.concatenate([q_ref[h * per + c] for c in range(per)], axis=1)
        else:
            qh = q_ref[:, sl]
        q = _rms(qh, qg_ref[...]).astype(BF16)
        s = _dot_nt(q, k_ref[:, sl]) * scale
        e = jnp.exp(s - jnp.max(s, axis=-1, keepdims=True))
        p = e / jnp.sum(e, axis=-1, keepdims=True)
        o_ref[:, sl] = _dot(p.astype(BF16), v_ref[:, sl]).astype(o_ref.dtype)


def _memattn(z, q_col, mk, mv, qk_gain, layer, S, tm):
    head_major = z.ndim == 3
    T = z.shape[1] if head_major else z.shape[0]
    _, M, mw = mk.shape
    hd = mw // MEM_HEADS
    qb = q_col // mw
    assert qb * mw == q_col and hd % LANES == 0
    per_b = S // tm
    if head_major:
        q_spec = pl.BlockSpec((mw // LANES, tm, LANES), lambda i: (qb, i, 0))
    else:
        q_spec = pl.BlockSpec((tm, mw), lambda i: (i, qb))
    kern = functools.partial(_memattn_kernel, heads=MEM_HEADS, hd=hd, head_major=head_major)
    return pl.pallas_call(
        kern,
        grid=(T // tm,),
        in_specs=[
            q_spec,
            pl.BlockSpec((None, None, 1, hd), lambda i: (layer, 0, 0, 0)),
            pl.BlockSpec((None, M, mw), lambda i: (i // per_b, 0, 0)),
            pl.BlockSpec((None, M, mw), lambda i: (i // per_b, 0, 0)),
        ],
        out_specs=pl.BlockSpec((tm, mw), lambda i: (i, 0)),
        out_shape=jax.ShapeDtypeStruct((T, mw), BF16),
        compiler_params=_params("parallel"),
        name="memattn",
    )(z, qk_gain, mk, mv)


def _outproj_kernel(ya_ref, ym_ref, wa_ref, wm_ref, x_ref, o_ref):
    o_ref[...] = x_ref[...] + _dot(ya_ref[...], wa_ref[...]) + _dot(ym_ref[...], wm_ref[...])


def _outproj_heads_kernel(ya_ref, ym_ref, wa_ref, wm_ref, x_ref, o_ref, cat_ref):
    @pl.when(pl.program_id(1) == 0)
    def _():
        for h in range(ya_ref.shape[0]):
            cat_ref[:, h * LANES:(h + 1) * LANES] = ya_ref[h]

    o_ref[...] = x_ref[...] + _dot(cat_ref[...], wa_ref[...]) + _dot(ym_ref[...], wm_ref[...])


def _outproj(x, ya, ym, w, idx, tm, tn):
    T, D = x.shape
    head_major = ya.ndim == 3
    ka = ya.shape[0] * LANES if head_major else ya.shape[1]
    km = ym.shape[1]
    mb = ka // km
    assert mb * km == ka
    if head_major:
        ya_spec = pl.BlockSpec((ka // LANES, tm, LANES), lambda i, j: (0, i, 0))
        kern, scratch = _outproj_heads_kernel, [pltpu.VMEM((tm, ka), BF16)]
    else:
        ya_spec = pl.BlockSpec((tm, ka), lambda i, j: (i, 0))
        kern, scratch = _outproj_kernel, []
    return pl.pallas_call(
        kern,
        grid=(T // tm, D // tn),
        in_specs=[
            ya_spec,
            pl.BlockSpec((tm, km), lambda i, j: (i, 0)),
            pl.BlockSpec((None, ka, tn), lambda i, j: (idx, 0, j)),
            pl.BlockSpec((None, km, tn), lambda i, j: (idx, mb, j)),
            pl.BlockSpec((tm, tn), lambda i, j: (i, j)),
        ],
        out_specs=pl.BlockSpec((tm, tn), lambda i, j: (i, j)),
        out_shape=jax.ShapeDtypeStruct((T, D), F32),
        scratch_shapes=scratch,
        compiler_params=_params("parallel", "arbitrary"),
        name="outproj",
    )(ya, ym, w, w, x)


def _rec_kernel(*refs, reverse, final, ts, bw):
    if final:
        (prev_ref, main_ref, next_ref, cw_ref, cb_ref, gw_ref, gb_ref, lam_ref,
         gate_ref, hb_ref, o_ref, xe_ref, a_ref, u_ref, h_ref) = refs
    else:
        (prev_ref, main_ref, next_ref, cw_ref, cb_ref, gw_ref, gb_ref, lam_ref,
         o_ref, xe_ref, a_ref, u_ref, h_ref) = refs
    c = pl.program_id(1)
    nc = pl.num_programs(1)
    chunk = nc - 1 - c if reverse else c

    @pl.when(c == 0)
    def _():
        h_ref[...] = jnp.zeros_like(h_ref)

    xe_ref[0:SUBLANES, :] = jnp.where(chunk > 0, prev_ref[...], 0.0)
    xe_ref[SUBLANES:SUBLANES + ts, :] = main_ref[...]
    xe_ref[SUBLANES + ts:2 * SUBLANES + ts, :] = jnp.where(chunk < nc - 1, next_ref[...], 0.0)
    xc = cb_ref[...]
    for k in range(CONV_WIDTH):
        xc = xc + cw_ref[k:k + 1, :] * xe_ref[pl.ds(SUBLANES - CONV_LEFT + k, ts), :]

    softplus_neg_lam = jax.nn.softplus(-lam_ref[...])
    xcb = xc.astype(BF16)
    for n in range(LRU_BLOCKS):
        sl = slice(n * bw, (n + 1) * bw)
        r = jax.nn.sigmoid(_dot(xcb[:, sl], gw_ref[0, n]) + gb_ref[0:1, sl])
        i = jax.nn.sigmoid(_dot(xcb[:, sl], gw_ref[1, n]) + gb_ref[1:2, sl])
        log_a = -LRU_C * r * softplus_neg_lam[:, sl]
        a = jnp.exp(log_a)
        a_ref[:, sl] = a
        u_ref[:, sl] = jnp.sqrt(1.0 - a * a) * (i * xc[:, sl])

    def step(t, h):
        row = ts - 1 - t if reverse else t
        h = a_ref[pl.ds(row, 1), :] * h + u_ref[pl.ds(row, 1), :]
        u_ref[pl.ds(row, 1), :] = h
        return h

    h_ref[...] = lax.fori_loop(0, ts, step, h_ref[...], unroll=8)

    if final:
        h_both = u_ref[...] + hb_ref[...]
        o_ref[...] = (h_both * jax.nn.gelu(gate_ref[...])).astype(o_ref.dtype)
    else:
        o_ref[...] = u_ref[...]


def _rec_scan(z, conv_w, conv_b, gate_w, gate_b, lam, j, direction, h_other, S, ts):
    T = z.shape[0]
    W = conv_w.shape[2]
    bw = W // LRU_BLOCKS
    nc = S // ts
    sub = ts // SUBLANES
    reverse = direction == 1
    final = h_other is not None

    def chunk_of(c):
        return nc - 1 - c if reverse else c

    def main_map(b, c):
        return (b * nc + chunk_of(c), 1)

    def prev_map(b, c):
        return (jnp.maximum((b * nc + chunk_of(c)) * sub - 1, 0), 1)

    def next_map(b, c):
        return (jnp.minimum((b * nc + chunk_of(c) + 1) * sub, T // SUBLANES - 1), 1)

    in_specs = [
        pl.BlockSpec((SUBLANES, W), prev_map),
        pl.BlockSpec((ts, W), main_map),
        pl.BlockSpec((SUBLANES, W), next_map),
        pl.BlockSpec((None, CONV_WIDTH, W), lambda b, c: (j, 0, 0)),
        pl.BlockSpec((None, 1, W), lambda b, c: (j, 0, 0)),
        pl.BlockSpec((None, None, 2, LRU_BLOCKS, bw, bw), lambda b, c: (j, direction, 0, 0, 0, 0)),
        pl.BlockSpec((None, None, 2, W), lambda b, c: (j, direction, 0, 0)),
        pl.BlockSpec((None, None, 1, W), lambda b, c: (j, direction, 0, 0)),
    ]
    args = [z, z, z, conv_w, conv_b, gate_w, gate_b, lam]
    if final:
        in_specs += [
            pl.BlockSpec((ts, W), lambda b, c: (b * nc + chunk_of(c), 0)),
            pl.BlockSpec((ts, W), lambda b, c: (b * nc + chunk_of(c), 0)),
        ]
        args += [z, h_other]
    kern = functools.partial(_rec_kernel, reverse=reverse, final=final, ts=ts, bw=bw)
    return pl.pallas_call(
        kern,
        grid=(T // S, nc),
        in_specs=in_specs,
        out_specs=pl.BlockSpec((ts, W), lambda b, c: (b * nc + chunk_of(c), 0)),
        out_shape=jax.ShapeDtypeStruct((T, W), BF16 if final else F32),
        scratch_shapes=[
            pltpu.VMEM((ts + 2 * SUBLANES, W), F32),
            pltpu.VMEM((ts, W), F32),
            pltpu.VMEM((ts, W), F32),
            pltpu.VMEM((1, W), F32),
        ],
        compiler_params=_params("parallel", "arbitrary"),
        name="rec_fwd" if final else "rec_bwd",
    )(*args)


ATT_TQ = 128
ATT_INTERLEAVE = 4


def _alibi_slopes():
    n = ATT_GROUPS * ATT_HEADS
    s = [2.0 ** (-8.0 * (i + 1.0) / n) for i in range(n)]
    return np.asarray(s, np.float32).reshape(ATT_GROUPS, ATT_HEADS)


def _rows(start, size, stride):
    if stride > 1:
        return pl.ds(start, size, stride=stride)
    return pl.ds(start if isinstance(start, int) else pl.multiple_of(start, SUBLANES), size)


def _attn_kernel(slopes_ref, *refs, span, halves):
    o_ref, acc_o, acc_l = refs[7 * ATT_GROUPS:]
    head = pl.program_id(1)
    blk = pl.program_id(2)
    nblk = pl.num_programs(2)
    tq, hd = ATT_TQ, ATT_HEAD_DIM
    half = halves[0]
    nk = tq + 2 * half
    row = lax.broadcasted_iota(jnp.int32, (tq, nk), 0)
    col = lax.broadcasted_iota(jnp.int32, (tq, nk), 1)
    rel = jnp.abs(col - half - row)
    in_band = rel <= half
    rel_f = rel.astype(F32)
    col1 = lax.broadcasted_iota(jnp.int32, (1, nk), 1)
    lo_mask = jnp.where((col1 < half) & (blk == 0), NEG_INF, 0.0)
    hi_mask = jnp.where((col1 >= nk - half) & (blk == nblk - 1), NEG_INF, 0.0)
    scale = hd ** -0.5

    for g, (_, d) in enumerate(ATT_PATTERNS):
        assert halves[g] == half
        q_ref, kp_ref, kc_ref, kn_ref, vp_ref, vc_ref, vn_ref = refs[7 * g:7 * g + 7]
        nu = span // (tq * d)
        slope = slopes_ref[g, head]
        bias_mid = jnp.where(in_band, -slope * (d * rel_f), NEG_INF)

        def tiles(specs, g=g, d=d, q_ref=q_ref, kp_ref=kp_ref, kc_ref=kc_ref, kn_ref=kn_ref,
                  vp_ref=vp_ref, vc_ref=vc_ref, vn_ref=vn_ref, bias_mid=bias_mid):
            done = []
            for r, u, kind in specs:
                qrows = _rows(u * (tq * d) + r, tq, d)
                lo = (u * tq - half) * d + r
                if kind == "only":
                    pieces = [(kp_ref, vp_ref, _rows(r, half, d)), (kc_ref, vc_ref, _rows(r, tq, d)),
                              (kn_ref, vn_ref, _rows(r, half, d))]
                    bias = bias_mid + lo_mask + hi_mask
                elif kind == "first":
                    pieces = [(kp_ref, vp_ref, _rows(r, half, d)), (kc_ref, vc_ref, _rows(r, tq + half, d))]
                    bias = bias_mid + lo_mask
                elif kind == "last":
                    pieces = [(kc_ref, vc_ref, _rows(lo, tq + half, d)), (kn_ref, vn_ref, _rows(r, half, d))]
                    bias = bias_mid + hi_mask
                else:
                    pieces = [(kc_ref, vc_ref, _rows(lo, nk, d))]
                    bias = bias_mid
                k = jnp.concatenate([kr[rows, :] for kr, _, rows in pieces], axis=0).astype(BF16)
                v = jnp.concatenate([vr[rows, :] for _, vr, rows in pieces], axis=0).astype(BF16)
                q = q_ref[qrows, :].astype(BF16)
                s = _dot_nt(q, k) * scale + bias
                m = jnp.max(s, axis=-1, keepdims=True)
                e = jnp.exp(s - m)
                den = jnp.sum(e, axis=-1, keepdims=True)
                o = _dot((e / den).astype(BF16), v)
                lse = jnp.broadcast_to(m + jnp.log(den), (tq, hd))
                prev = (acc_o[qrows, :], acc_l[qrows, :]) if g > 0 else None
                done.append((qrows, o, lse, prev))
            for qrows, o, lse, prev in done:
                if prev is not None:
                    o_acc, lse_acc = prev
                    top = jnp.maximum(lse_acc, lse)
                    w_acc = jnp.exp(lse_acc - top)
                    w_cur = jnp.exp(lse - top)
                    tot = w_acc + w_cur
                    o = (w_acc * o_acc + w_cur * o) / tot
                    lse = top + jnp.log(tot)
                acc_o[qrows, :] = o
                if g < ATT_GROUPS - 1:
                    acc_l[qrows, :] = lse

        def kind_of(u, nu=nu):
            return "only" if nu == 1 else "first" if u == 0 else "last" if u == nu - 1 else "mid"

        ilv = ATT_INTERLEAVE
        if nu >= ilv:
            assert nu % ilv == 0
            nchunk = nu // ilv

            def per_class(r, nu=nu, nchunk=nchunk, tiles=tiles, kind_of=kind_of):
                tiles([(r, u, kind_of(u)) for u in range(ilv)])
                if nchunk > 2:
                    def mid(c, carry):
                        tiles([(r, c * ilv + t, "mid") for t in range(ilv)])
                        return carry
                    lax.fori_loop(1, nchunk - 1, mid, 0)
                if nchunk > 1:
                    tiles([(r, u, kind_of(u)) for u in range(nu - ilv, nu)])

            if d == 1:
                per_class(0)
            else:
                def classes(r, carry, per_class=per_class):
                    per_class(r)
                    return carry
                lax.fori_loop(0, d, classes, 0)
        else:
            per_blk = ilv // nu
            assert per_blk * nu == ilv and d % per_blk == 0

            def classes(rb, carry, nu=nu, per_blk=per_blk, tiles=tiles, kind_of=kind_of):
                tiles([(rb * per_blk + t, u, kind_of(u)) for t in range(per_blk) for u in range(nu)])
                return carry
            lax.fori_loop(0, d // per_blk, classes, 0)

    o_ref[...] = acc_o[...].astype(o_ref.dtype)


def _attention(zh, B, S, span):
    T = zh.shape[1]
    nb = S // span
    halves = tuple(w // (2 * d) for w, d in ATT_PATTERNS)
    in_specs = [pl.BlockSpec(memory_space=pltpu.SMEM)]
    for g, (_, d) in enumerate(ATT_PATTERNS):
        halo = halves[g] * d
        assert span % (ATT_TQ * d) == 0 and span % halo == 0 and S % span == 0
        ratio = span // halo

        def slab(p, g=g):
            return (3 * g + p) * ATT_HEADS

        def cur(p, slab=slab):
            return pl.BlockSpec((None, span, LANES), lambda b, h, i: (slab(p) + h, b * nb + i, 0))

        def before(p, slab=slab, halo=halo, ratio=ratio):
            return pl.BlockSpec((None, halo, LANES),
                                lambda b, h, i: (slab(p) + h, jnp.maximum((b * nb + i) * ratio - 1, 0), 0))

        def after(p, slab=slab, halo=halo, ratio=ratio):
            return pl.BlockSpec((None, halo, LANES),
                                lambda b, h, i: (slab(p) + h, jnp.minimum((b * nb + i + 1) * ratio, T // halo - 1), 0))

        in_specs += [cur(0), before(1), cur(1), after(1), before(2), cur(2), after(2)]
    kern = functools.partial(_attn_kernel, span=span, halves=halves)
    return pl.pallas_call(
        kern,
        grid=(B, ATT_HEADS, nb),
        in_specs=in_specs,
        out_specs=pl.BlockSpec((None, span, LANES), lambda b, h, i: (h, b * nb + i, 0)),
        out_shape=jax.ShapeDtypeStruct((ATT_HEADS, T, LANES), BF16),
        scratch_shapes=[pltpu.VMEM((span, LANES), F32), pltpu.VMEM((span, LANES), F32)],
        compiler_params=_params("parallel", "parallel", "arbitrary"),
        name="attn",
    )(jnp.asarray(_alibi_slopes()), *([zh] * (7 * ATT_GROUPS)))


def kernel(x, mem, ffn_norm, ffn_w_in, ffn_w_out, mix_norm, mem_norm, mem_w_kv, mem_qk_gain,
           rec_w_in, rec_conv_w, rec_conv_b, rec_gate_w, rec_gate_b, rec_lambda, rec_w_out,
           att_w_in, att_qk_gain, att_w_out):
    B, S, D = x.shape
    T = B * S
    depth = ffn_norm.shape[0]
    W = rec_conv_w.shape[2]

    tm_ffn = min(512, S)
    tf = min(512, ffn_w_out.shape[2])
    tm_proj = min(1024, S)
    tn_proj = min(512, D)
    ts_rec = min(256, S)
    span = ATT_TQ * max(d for _, d in ATT_PATTERNS)

    ffn_w_in, ffn_w_out, mem_w_kv, rec_w_in, rec_gate_w, rec_w_out, att_w_in, att_w_out = (
        w.astype(BF16) for w in (ffn_w_in, ffn_w_out, mem_w_kv, rec_w_in, rec_gate_w, rec_w_out,
                                 att_w_in, att_w_out))
    ffn_norm = ffn_norm[:, :, None, :]
    mix_norm = mix_norm[:, None, :]
    mem_norm = mem_norm[:, None, :]
    mem_qk_gain = mem_qk_gain[:, :, None, :]
    rec_conv_b = rec_conv_b[:, None, :]
    rec_gate_b = rec_gate_b.reshape(rec_gate_b.shape[0], 2, 2, W)
    rec_lambda = rec_lambda[:, :, None, :]
    att_qk_gain = att_qk_gain[:, :, :, None, :]

    x = x.reshape(T, D)
    for layer in range(depth):
        x = _ffn(x, ffn_norm, ffn_w_in, ffn_w_out, layer, 0, tm_ffn, tf)
        mk, mv = _memkv(mem, mem_norm, mem_w_kv, mem_qk_gain, layer)
        j = layer // 2
        if layer % 2 == 0:
            z = _inproj(x, mix_norm, rec_w_in, (layer, j), tm_proj, tn_proj)
            hb = _rec_scan(z, rec_conv_w, rec_conv_b, rec_gate_w, rec_gate_b, rec_lambda, j, 1, None, S, ts_rec)
            ya = _rec_scan(z, rec_conv_w, rec_conv_b, rec_gate_w, rec_gate_b, rec_lambda, j, 0, hb, S, ts_rec)
            ym = _memattn(z, 2 * W, mk, mv, mem_qk_gain, layer, S, tm_proj)
            x = _outproj(x, ya, ym, rec_w_out, j, tm_proj, tn_proj)
        else:
            zh = _inproj_heads(x, mix_norm, att_w_in, att_qk_gain, (layer, j), tm_proj, tn_proj)
            ya = _attention(zh, B, S, span)
            ym = _memattn(zh, ATT_GROUPS * 3 * ATT_WIDTH, mk, mv, mem_qk_gain, layer, S, tm_proj)
            x = _outproj(x, ya, ym, att_w_out, j, tm_proj, tn_proj)
        x = _ffn(x, ffn_norm, ffn_w_in, ffn_w_out, layer, 1, tm_ffn, tf)
    return x.reshape(B, S, D)
```

```python
import functools

import jax
import jax.numpy as jnp
import numpy as np
from jax import lax
from jax.experimental import pallas as pl
from jax.experimental.pallas import tpu as pltpu

F32 = jnp.float32
BF16 = jnp.bfloat16

EPS = 1e-6
NEG_INF = -1e30
LRU_C = 8.0
LRU_BLOCKS = 8
CONV_WIDTH = 4
CONV_LEFT = CONV_WIDTH // 2
ATT_PATTERNS = ((128, 1), (512, 4), (2048, 16))
ATT_GROUPS = len(ATT_PATTERNS)
ATT_HEADS = 8
ATT_HEAD_DIM = 128
ATT_WIDTH = ATT_HEADS * ATT_HEAD_DIM
MEM_HEADS = 4
SUBLANES = 8
LANES = 128
VMEM_LIMIT_BYTES = 56 * 1024 * 1024


def _params(*semantics):
    return pltpu.CompilerParams(dimension_semantics=semantics, vmem_limit_bytes=VMEM_LIMIT_BYTES)


def _rms(x, g):
    return x * lax.rsqrt(jnp.mean(x * x, axis=-1, keepdims=True) + EPS) * g


def _dot(a, b):
    return jnp.dot(a, b, preferred_element_type=F32)


def _dot_nt(a, b):
    return lax.dot_general(a, b, (((1,), (1,)), ((), ())), preferred_element_type=F32)


def _ffn_kernel(x_ref, g_ref, wg_ref, wu_ref, wo_ref, o_ref, xn_ref, acc_ref):
    j = pl.program_id(1)

    @pl.when(j == 0)
    def _():
        xn_ref[...] = _rms(x_ref[...], g_ref[...]).astype(BF16)
        acc_ref[...] = jnp.zeros_like(acc_ref)

    xn = xn_ref[...]
    gate = _dot(xn, wg_ref[...])
    up = _dot(xn, wu_ref[...])
    h = (jax.nn.silu(gate) * up).astype(BF16)
    acc_ref[...] += _dot(h, wo_ref[...])

    @pl.when(j == pl.num_programs(1) - 1)
    def _():
        o_ref[...] = x_ref[...] + 0.5 * acc_ref[...]


def _ffn(x, g, w_in, w_out, layer, k, tm, tf):
    T, D = x.shape
    F = w_out.shape[2]
    nf = F // tf
    return pl.pallas_call(
        _ffn_kernel,
        grid=(T // tm, nf),
        in_specs=[
            pl.BlockSpec((tm, D), lambda i, j: (i, 0)),
            pl.BlockSpec((None, None, 1, D), lambda i, j: (layer, k, 0, 0)),
            pl.BlockSpec((None, None, D, tf), lambda i, j: (layer, k, 0, j)),
            pl.BlockSpec((None, None, D, tf), lambda i, j: (layer, k, 0, j + nf)),
            pl.BlockSpec((None, None, tf, D), lambda i, j: (layer, k, j, 0)),
        ],
        out_specs=pl.BlockSpec((tm, D), lambda i, j: (i, 0)),
        out_shape=jax.ShapeDtypeStruct((T, D), F32),
        scratch_shapes=[pltpu.VMEM((tm, D), BF16), pltpu.VMEM((tm, D), F32)],
        compiler_params=_params("parallel", "arbitrary"),
        name="ffn",
    )(x, g, w_in, w_in, w_out)


def _segment_major_perm(chunk, inverse):
    seg = chunk // SUBLANES
    i = lax.broadcasted_iota(jnp.int32, (chunk, chunk), 0)
    j = lax.broadcasted_iota(jnp.int32, (chunk, chunk), 1)
    src = (i % seg) * SUBLANES + i // seg if inverse else (i % SUBLANES) * seg + i // SUBLANES
    return jnp.where(j == src, 1.0, 0.0).astype(BF16)


def _permute_chunks(dst_ref, src, chunk, inverse):
    perm = _segment_major_perm(chunk, inverse)
    for c in range(src.shape[0] // chunk):
        rows = slice(c * chunk, (c + 1) * chunk)
        dst_ref[rows, :] = _dot(perm, src[rows, :]).astype(BF16)


def _inproj_kernel(x_ref, g_ref, w_ref, o_ref, xn_ref, *, chunk):
    @pl.when(pl.program_id(1) == 0)
    def _():
        xn = _rms(x_ref[...], g_ref[...]).astype(BF16)
        _permute_chunks(xn_ref, xn, chunk, inverse=False)

    o_ref[...] = _dot(xn_ref[...], w_ref[...])


def _inproj(x, g, w, idx, tm, tn, chunk):
    T, D = x.shape
    N = w.shape[2]
    assert tm % chunk == 0
    return pl.pallas_call(
        functools.partial(_inproj_kernel, chunk=chunk),
        grid=(T // tm, N // tn),
        in_specs=[
            pl.BlockSpec((tm, D), lambda i, j: (i, 0)),
            pl.BlockSpec((None, 1, D), lambda i, j: (idx[0], 0, 0)),
            pl.BlockSpec((None, D, tn), lambda i, j: (idx[1], 0, j)),
        ],
        out_specs=pl.BlockSpec((tm, tn), lambda i, j: (i, j)),
        out_shape=jax.ShapeDtypeStruct((T, N), F32),
        scratch_shapes=[pltpu.VMEM((tm, D), BF16)],
        compiler_params=_params("parallel", "arbitrary"),
        name="inproj",
    )(x, g, w)


def _inproj_heads_kernel(x_ref, g_ref, w_ref, qkg_ref, o_ref, xn_ref, *, tiles_per_part):
    j = pl.program_id(1)

    @pl.when(j == 0)
    def _():
        xn_ref[...] = _rms(x_ref[...], g_ref[...]).astype(BF16)

    z = _dot(xn_ref[...], w_ref[...])
    part = j // tiles_per_part
    is_qk = (part < 3 * ATT_GROUPS) & (part % 3 < 2)
    heads = o_ref.shape[0]

    @pl.when(is_qk)
    def _():
        for h in range(heads):
            o_ref[h] = _rms(z[:, h * LANES:(h + 1) * LANES], qkg_ref[...])

    @pl.when(jnp.logical_not(is_qk))
    def _():
        for h in range(heads):
            o_ref[h] = z[:, h * LANES:(h + 1) * LANES]


def _inproj_heads(x, g, w, qk_gain, idx, tm, tn):
    T, D = x.shape
    N = w.shape[2]
    assert ATT_HEAD_DIM == LANES and tn % LANES == 0 and ATT_WIDTH % tn == 0
    tpp = ATT_WIDTH // tn

    def gain_map(i, j):
        part = j // tpp
        return (idx[1], jnp.minimum(part % 3, 1), jnp.minimum(part // 3, ATT_GROUPS - 1), 0, 0)

    kern = functools.partial(_inproj_heads_kernel, tiles_per_part=tpp)
    return pl.pallas_call(
        kern,
        grid=(T // tm, N // tn),
        in_specs=[
            pl.BlockSpec((tm, D), lambda i, j: (i, 0)),
            pl.BlockSpec((None, 1, D), lambda i, j: (idx[0], 0, 0)),
            pl.BlockSpec((None, D, tn), lambda i, j: (idx[1], 0, j)),
            pl.BlockSpec((None, None, None, 1, LANES), gain_map),
        ],
        out_specs=pl.BlockSpec((tn // LANES, tm, LANES), lambda i, j: (j, i, 0)),
        out_shape=jax.ShapeDtypeStruct((N // LANES, T, LANES), F32),
        scratch_shapes=[pltpu.VMEM((tm, D), BF16)],
        compiler_params=_params("parallel", "arbitrary"),
        name="inproj_heads",
    )(x, g, w, qk_gain)


def _memkv_kernel(mem_ref, g_ref, w_ref, kg_ref, k_ref, v_ref, *, heads, hd):
    mem_n = _rms(mem_ref[...], g_ref[...]).astype(BF16)
    kv = _dot(mem_n, w_ref[...])
    for h in range(heads):
        sl = slice(h * hd, (h + 1) * hd)
        k_ref[:, sl] = _rms(kv[:, sl], kg_ref[...]).astype(BF16)
    v_ref[...] = kv[:, heads * hd:].astype(BF16)


def _memkv(mem, g, w_kv, qk_gain, layer):
    B, M, D = mem.shape
    mw = w_kv.shape[2] // 2
    hd = mw // MEM_HEADS
    kern = functools.partial(_memkv_kernel, heads=MEM_HEADS, hd=hd)
    return pl.pallas_call(
        kern,
        grid=(B,),
        in_specs=[
            pl.BlockSpec((None, M, D), lambda b: (b, 0, 0)),
            pl.BlockSpec((None, 1, D), lambda b: (layer, 0, 0)),
            pl.BlockSpec((None, D, 2 * mw), lambda b: (layer, 0, 0)),
            pl.BlockSpec((None, None, 1, hd), lambda b: (layer, 1, 0, 0)),
        ],
        out_specs=[pl.BlockSpec((None, M, mw), lambda b: (b, 0, 0))] * 2,
        out_shape=[jax.ShapeDtypeStruct((B, M, mw), BF16)] * 2,
        compiler_params=_params("parallel"),
        name="memkv",
    )(mem, g, w_kv, qk_gain)


def _memattn_kernel(q_ref, qg_ref, k_ref, v_ref, o_ref, *, heads, hd, head_major):
    scale = hd ** -0.5
    per = hd // LANES
    for h in range(heads):
        sl = slice(h * hd, (h + 1) * hd)
        if head_major:
            qh = jnp.concatenate([q_ref[h * per + c] for c in range(per)], axis=1)
        else:
            qh = q_ref[:, sl]
        q = _rms(qh, qg_ref[...]).astype(BF16)
        s = _dot_nt(q, k_ref[:, sl]) * scale
        e = jnp.exp(s - jnp.max(s, axis=-1, keepdims=True))
        p = e / jnp.sum(e, axis=-1, keepdims=True)
        o_ref[:, sl] = _dot(p.astype(BF16), v_ref[:, sl]).astype(o_ref.dtype)


def _memattn(z, q_col, mk, mv, qk_gain, layer, S, tm):
    head_major = z.ndim == 3
    T = z.shape[1] if head_major else z.shape[0]
    _, M, mw = mk.shape
    hd = mw // MEM_HEADS
    qb = q_col // mw
    assert qb * mw == q_col and hd % LANES == 0
    per_b = S // tm
    if head_major:
        q_spec = pl.BlockSpec((mw // LANES, tm, LANES), lambda i: (qb, i, 0))
    else:
        q_spec = pl.BlockSpec((tm, mw), lambda i: (i, qb))
    kern = functools.partial(_memattn_kernel, heads=MEM_HEADS, hd=hd, head_major=head_major)
    return pl.pallas_call(
        kern,
        grid=(T // tm,),
        in_specs=[
            q_spec,
            pl.BlockSpec((None, None, 1, hd), lambda i: (layer, 0, 0, 0)),
            pl.BlockSpec((None, M, mw), lambda i: (i // per_b, 0, 0)),
            pl.BlockSpec((None, M, mw), lambda i: (i // per_b, 0, 0)),
        ],
        out_specs=pl.BlockSpec((tm, mw), lambda i: (i, 0)),
        out_shape=jax.ShapeDtypeStruct((T, mw), BF16),
        compiler_params=_params("parallel"),
        name="memattn",
    )(z, qk_gain, mk, mv)


def _outproj_kernel(ya_ref, ym_ref, wa_ref, wm_ref, x_ref, o_ref, ya_seq, ym_seq, *, chunk):
    @pl.when(pl.program_id(1) == 0)
    def _():
        _permute_chunks(ya_seq, ya_ref[...], chunk, inverse=True)
        _permute_chunks(ym_seq, ym_ref[...], chunk, inverse=True)

    o_ref[...] = x_ref[...] + _dot(ya_seq[...], wa_ref[...]) + _dot(ym_seq[...], wm_ref[...])


def _outproj_heads_kernel(ya_ref, ym_ref, wa_ref, wm_ref, x_ref, o_ref, cat_ref):
    @pl.when(pl.program_id(1) == 0)
    def _():
        for h in range(ya_ref.shape[0]):
            cat_ref[:, h * LANES:(h + 1) * LANES] = ya_ref[h]

    o_ref[...] = x_ref[...] + _dot(cat_ref[...], wa_ref[...]) + _dot(ym_ref[...], wm_ref[...])


def _outproj(x, ya, ym, w, idx, tm, tn, chunk=None):
    T, D = x.shape
    head_major = ya.ndim == 3
    assert head_major == (chunk is None)
    ka = ya.shape[0] * LANES if head_major else ya.shape[1]
    km = ym.shape[1]
    mb = ka // km
    assert mb * km == ka
    if head_major:
        ya_spec = pl.BlockSpec((ka // LANES, tm, LANES), lambda i, j: (0, i, 0))
        kern, scratch = _outproj_heads_kernel, [pltpu.VMEM((tm, ka), BF16)]
    else:
        assert tm % chunk == 0
        ya_spec = pl.BlockSpec((tm, ka), lambda i, j: (i, 0))
        kern = functools.partial(_outproj_kernel, chunk=chunk)
        scratch = [pltpu.VMEM((tm, ka), BF16), pltpu.VMEM((tm, km), BF16)]
    return pl.pallas_call(
        kern,
        grid=(T // tm, D // tn),
        in_specs=[
            ya_spec,
            pl.BlockSpec((tm, km), lambda i, j: (i, 0)),
            pl.BlockSpec((None, ka, tn), lambda i, j: (idx, 0, j)),
            pl.BlockSpec((None, km, tn), lambda i, j: (idx, mb, j)),
            pl.BlockSpec((tm, tn), lambda i, j: (i, j)),
        ],
        out_specs=pl.BlockSpec((tm, tn), lambda i, j: (i, j)),
        out_shape=jax.ShapeDtypeStruct((T, D), F32),
        scratch_shapes=scratch,
        compiler_params=_params("parallel", "arbitrary"),
        name="outproj",
    )(ya, ym, w, w, x)


def _sigmoid(x):
    return 0.5 * jnp.tanh(0.5 * x) + 0.5


def _rec_kernel(*refs, reverse, final, ts, bw):
    if final:
        (prev_ref, main_ref, next_ref, cw_ref, cb_ref, gw_ref, gb_ref, lam_ref,
         gate_ref, hb_ref, o_ref, a_ref, u_ref, h_ref) = refs
    else:
        (prev_ref, main_ref, next_ref, cw_ref, cb_ref, gw_ref, gb_ref, lam_ref,
         o_ref, a_ref, u_ref, h_ref) = refs
    c = pl.program_id(1)
    nc = pl.num_programs(1)
    chunk = nc - 1 - c if reverse else c
    seg = ts // SUBLANES
    W = main_ref.shape[1]
    assert CONV_WIDTH == 4 and CONV_LEFT == 2

    @pl.when(c == 0)
    def _():
        h_ref[...] = jnp.zeros_like(h_ref)

    main = main_ref[...]
    prev = jnp.where(chunk > 0, prev_ref[...], 0.0)
    nxt = jnp.where(chunk < nc - 1, next_ref[...], 0.0)
    sub = lax.broadcasted_iota(jnp.int32, (SUBLANES, W), 0)

    def from_prev_segment(tile, fill):
        return jnp.where(sub == 0, fill, pltpu.roll(tile, 1, axis=0))

    def from_next_segment(tile, fill):
        return jnp.where(sub == SUBLANES - 1, fill, pltpu.roll(tile, SUBLANES - 1, axis=0))

    back1 = from_prev_segment(main[ts - SUBLANES:], prev[2 * SUBLANES - 1:])
    back2 = from_prev_segment(main[ts - 2 * SUBLANES:ts - SUBLANES], prev[SUBLANES - 1:SUBLANES])
    ahead1 = from_next_segment(main[:SUBLANES], nxt[0:1])
    taps = (jnp.concatenate([back2, back1, main[:ts - 2 * SUBLANES]], axis=0),
            jnp.concatenate([back1, main[:ts - SUBLANES]], axis=0),
            main,
            jnp.concatenate([main[SUBLANES:], ahead1], axis=0))
    xc = cb_ref[...]
    for k in range(CONV_WIDTH):
        xc = xc + cw_ref[k:k + 1, :] * taps[k]

    softplus_neg_lam = jax.nn.softplus(-lam_ref[...])
    xcb = xc.astype(BF16)
    for n in range(LRU_BLOCKS):
        sl = slice(n * bw, (n + 1) * bw)
        r = _sigmoid(_dot(xcb[:, sl], gw_ref[0, n]) + gb_ref[0:1, sl])
        i = _sigmoid(_dot(xcb[:, sl], gw_ref[1, n]) + gb_ref[1:2, sl])
        log_a = -LRU_C * r * softplus_neg_lam[:, sl]
        a = jnp.exp(log_a)
        y = 1.0 - a * a
        a_ref[:, sl] = a
        u_ref[:, sl] = (y * lax.rsqrt(jnp.maximum(y, 1e-30))) * (i * xc[:, sl])

    def step(t, carry):
        h, prod = carry
        k = seg - 1 - t if reverse else t
        rows = pl.ds(pl.multiple_of(k * SUBLANES, SUBLANES), SUBLANES)
        a = a_ref[rows, :]
        h = a * h + u_ref[rows, :]
        prod = a * prod
        u_ref[rows, :] = h
        a_ref[rows, :] = prod
        return h, prod

    h_end, a_end = lax.fori_loop(0, seg, step, (jnp.zeros((SUBLANES, W), F32), jnp.ones((SUBLANES, W), F32)),
                                 unroll=2)

    carry = h_ref[...]
    h_in = jnp.zeros((SUBLANES, W), F32)
    for s in (range(SUBLANES - 1, -1, -1) if reverse else range(SUBLANES)):
        h_in = jnp.where(sub == s, carry, h_in)
        carry = h_end[s:s + 1] + a_end[s:s + 1] * carry
    h_ref[...] = carry

    hs = u_ref[...].reshape(seg, SUBLANES, W) + a_ref[...].reshape(seg, SUBLANES, W) * h_in[None]
    hs = hs.reshape(ts, W)
    if final:
        o_ref[...] = ((hs + hb_ref[...]) * jax.nn.gelu(gate_ref[...])).astype(o_ref.dtype)
    else:
        o_ref[...] = hs


def _rec_scan(z, conv_w, conv_b, gate_w, gate_b, lam, j, direction, h_other, S, ts):
    T = z.shape[0]
    W = conv_w.shape[2]
    bw = W // LRU_BLOCKS
    nc = S // ts
    reverse = direction == 1
    final = h_other is not None
    assert ts % (2 * SUBLANES) == 0

    def chunk_of(c):
        return nc - 1 - c if reverse else c

    def main_map(b, c):
        return (b * nc + chunk_of(c), 1)

    def prev_map(b, c):
        return (jnp.maximum((b * nc + chunk_of(c)) * (ts // (2 * SUBLANES)) - 1, 0), 1)

    def next_map(b, c):
        return (jnp.minimum((b * nc + chunk_of(c) + 1) * (ts // SUBLANES), T // SUBLANES - 1), 1)

    in_specs = [
        pl.BlockSpec((2 * SUBLANES, W), prev_map),
        pl.BlockSpec((ts, W), main_map),
        pl.BlockSpec((SUBLANES, W), next_map),
        pl.BlockSpec((None, CONV_WIDTH, W), lambda b, c: (j, 0, 0)),
        pl.BlockSpec((None, 1, W), lambda b, c: (j, 0, 0)),
        pl.BlockSpec((None, None, 2, LRU_BLOCKS, bw, bw), lambda b, c: (j, direction, 0, 0, 0, 0)),
        pl.BlockSpec((None, None, 2, W), lambda b, c: (j, direction, 0, 0)),
        pl.BlockSpec((None, None, 1, W), lambda b, c: (j, direction, 0, 0)),
    ]
    args = [z, z, z, conv_w, conv_b, gate_w, gate_b, lam]
    if final:
        in_specs += [
            pl.BlockSpec((ts, W), lambda b, c: (b * nc + chunk_of(c), 0)),
            pl.BlockSpec((ts, W), lambda b, c: (b * nc + chunk_of(c), 0)),
        ]
        args += [z, h_other]
    kern = functools.partial(_rec_kernel, reverse=reverse, final=final, ts=ts, bw=bw)
    return pl.pallas_call(
        kern,
        grid=(T // S, nc),
        in_specs=in_specs,
        out_specs=pl.BlockSpec((ts, W), lambda b, c: (b * nc + chunk_of(c), 0)),
        out_shape=jax.ShapeDtypeStruct((T, W), BF16 if final else F32),
        scratch_shapes=[
            pltpu.VMEM((ts, W), F32),
            pltpu.VMEM((ts, W), F32),
            pltpu.VMEM((1, W), F32),
        ],
        compiler_params=_params("parallel", "arbitrary"),
        name="rec_fwd" if final else "rec_bwd",
    )(*args)


ATT_TQ = 128
ATT_INTERLEAVE = 16


def _alibi_slopes():
    n = ATT_GROUPS * ATT_HEADS
    s = [2.0 ** (-8.0 * (i + 1.0) / n) for i in range(n)]
    return np.asarray(s, np.float32).reshape(ATT_GROUPS, ATT_HEADS)


def _rows(start, size, stride):
    if stride > 1:
        return pl.ds(start, size, stride=stride)
    return pl.ds(start if isinstance(start, int) else pl.multiple_of(start, SUBLANES), size)


def _attn_kernel(slopes_ref, *refs, span, halves):
    o_ref, acc_o, acc_l = refs[7 * ATT_GROUPS:]
    head = pl.program_id(1)
    blk = pl.program_id(2)
    nblk = pl.num_programs(2)
    tq, hd = ATT_TQ, ATT_HEAD_DIM
    half = halves[0]
    nk = tq + 2 * half
    row = lax.broadcasted_iota(jnp.int32, (tq, nk), 0)
    col = lax.broadcasted_iota(jnp.int32, (tq, nk), 1)
    rel = jnp.abs(col - half - row)
    in_band = rel <= half
    rel_f = rel.astype(F32)
    col1 = lax.broadcasted_iota(jnp.int32, (1, nk), 1)
    lo_mask = jnp.where((col1 < half) & (blk == 0), NEG_INF, 0.0)
    hi_mask = jnp.where((col1 >= nk - half) & (blk == nblk - 1), NEG_INF, 0.0)
    scale = hd ** -0.5

    for g, (_, d) in enumerate(ATT_PATTERNS):
        assert halves[g] == half
        q_ref, kp_ref, kc_ref, kn_ref, vp_ref, vc_ref, vn_ref = refs[7 * g:7 * g + 7]
        nu = span // (tq * d)
        slope = slopes_ref[g, head]
        bias_mid = jnp.where(in_band, -slope * (d * rel_f), NEG_INF)
        biases = {"mid": bias_mid, "first": bias_mid + lo_mask, "last": bias_mid + hi_mask,
                  "only": bias_mid + lo_mask + hi_mask}

        def tiles(specs, g=g, d=d, q_ref=q_ref, kp_ref=kp_ref, kc_ref=kc_ref, kn_ref=kn_ref,
                  vp_ref=vp_ref, vc_ref=vc_ref, vn_ref=vn_ref, biases=biases):
            done = []
            for r, u, kind in specs:
                qrows = _rows(u * (tq * d) + r, tq, d)
                lo = (u * tq - half) * d + r
                if kind == "only":
                    pieces = [(kp_ref, vp_ref, _rows(r, half, d)), (kc_ref, vc_ref, _rows(r, tq, d)),
                              (kn_ref, vn_ref, _rows(r, half, d))]
                elif kind == "first":
                    pieces = [(kp_ref, vp_ref, _rows(r, half, d)), (kc_ref, vc_ref, _rows(r, tq + half, d))]
                elif kind == "last":
                    pieces = [(kc_ref, vc_ref, _rows(lo, tq + half, d)), (kn_ref, vn_ref, _rows(r, half, d))]
                else:
                    pieces = [(kc_ref, vc_ref, _rows(lo, nk, d))]
                k = jnp.concatenate([kr[rows, :] for kr, _, rows in pieces], axis=0).astype(BF16)
                v = jnp.concatenate([vr[rows, :] for _, vr, rows in pieces], axis=0).astype(BF16)
                q = q_ref[qrows, :].astype(BF16)
                s = _dot_nt(q, k) * scale + biases[kind]
                m = jnp.max(s, axis=-1, keepdims=True)
                e = jnp.exp(s - m)
                den = jnp.sum(e, axis=-1, keepdims=True)
                o = _dot((e / den).astype(BF16), v)
                lse = jnp.broadcast_to(m + jnp.log(den), (tq, hd))
                prev = (acc_o[qrows, :], acc_l[qrows, :]) if g > 0 else None
                done.append((qrows, o, lse, prev))
            for qrows, o, lse, prev in done:
                if prev is not None:
                    o_acc, lse_acc = prev
                    top = jnp.maximum(lse_acc, lse)
                    w_acc = jnp.exp(lse_acc - top)
                    w_cur = jnp.exp(lse - top)
                    tot = w_acc + w_cur
                    o = (w_acc * o_acc + w_cur * o) / tot
                    lse = top + jnp.log(tot)
                acc_o[qrows, :] = o
                if g < ATT_GROUPS - 1:
                    acc_l[qrows, :] = lse

        def kind_of(u, nu=nu):
            return "only" if nu == 1 else "first" if u == 0 else "last" if u == nu - 1 else "mid"

        ilv = ATT_INTERLEAVE
        if nu >= ilv:
            assert nu % ilv == 0
            nchunk = nu // ilv

            def per_class(r, nu=nu, nchunk=nchunk, tiles=tiles, kind_of=kind_of):
                tiles([(r, u, kind_of(u)) for u in range(ilv)])
                if nchunk > 2:
                    def mid(c, carry):
                        tiles([(r, c * ilv + t, "mid") for t in range(ilv)])
                        return carry
                    lax.fori_loop(1, nchunk - 1, mid, 0)
                if nchunk > 1:
                    tiles([(r, u, kind_of(u)) for u in range(nu - ilv, nu)])

            if d == 1:
                per_class(0)
            else:
                def classes(r, carry, per_class=per_class):
                    per_class(r)
                    return carry
                lax.fori_loop(0, d, classes, 0)
        else:
            per_blk = ilv // nu
            assert per_blk * nu == ilv and d % per_blk == 0

            def classes(rb, carry, nu=nu, per_blk=per_blk, tiles=tiles, kind_of=kind_of):
                tiles([(rb * per_blk + t, u, kind_of(u)) for t in range(per_blk) for u in range(nu)])
                return carry
            lax.fori_loop(0, d // per_blk, classes, 0)

    o_ref[...] = acc_o[...].astype(o_ref.dtype)


def _attention(zh, B, S, span):
    T = zh.shape[1]
    nb = S // span
    halves = tuple(w // (2 * d) for w, d in ATT_PATTERNS)
    in_specs = [pl.BlockSpec(memory_space=pltpu.SMEM)]
    for g, (_, d) in enumerate(ATT_PATTERNS):
        halo = halves[g] * d
        assert span % (ATT_TQ * d) == 0 and span % halo == 0 and S % span == 0
        ratio = span // halo

        def slab(p, g=g):
            return (3 * g + p) * ATT_HEADS

        def cur(p, slab=slab):
            return pl.BlockSpec((None, span, LANES), lambda b, h, i: (slab(p) + h, b * nb + i, 0))

        def before(p, slab=slab, halo=halo, ratio=ratio):
            return pl.BlockSpec((None, halo, LANES),
                                lambda b, h, i: (slab(p) + h, jnp.maximum((b * nb + i) * ratio - 1, 0), 0))

        def after(p, slab=slab, halo=halo, ratio=ratio):
            return pl.BlockSpec((None, halo, LANES),
                                lambda b, h, i: (slab(p) + h, jnp.minimum((b * nb + i + 1) * ratio, T // halo - 1), 0))

        in_specs += [cur(0), before(1), cur(1), after(1), before(2), cur(2), after(2)]
    kern = functools.partial(_attn_kernel, span=span, halves=halves)
    return pl.pallas_call(
        kern,
        grid=(B, ATT_HEADS, nb),
        in_specs=in_specs,
        out_specs=pl.BlockSpec((None, span, LANES), lambda b, h, i: (h, b * nb + i, 0)),
        out_shape=jax.ShapeDtypeStruct((ATT_HEADS, T, LANES), BF16),
        scratch_shapes=[pltpu.VMEM((span, LANES), F32), pltpu.VMEM((span, LANES), F32)],
        compiler_params=_params("parallel", "parallel", "arbitrary"),
        name="attn",
    )(jnp.asarray(_alibi_slopes()), *([zh] * (7 * ATT_GROUPS)))


def kernel(x, mem, ffn_norm, ffn_w_in, ffn_w_out, mix_norm, mem_norm, mem_w_kv, mem_qk_gain,
           rec_w_in, rec_conv_w, rec_conv_b, rec_gate_w, rec_gate_b, rec_lambda, rec_w_out,
           att_w_in, att_qk_gain, att_w_out):
    B, S, D = x.shape
    T = B * S
    depth = ffn_norm.shape[0]
    W = rec_conv_w.shape[2]

    tm_ffn = min(512, S)
    tf = min(512, ffn_w_out.shape[2])
    tm_proj = min(1024, S)
    tn_proj = min(512, D)
    ts_rec = min(256, S)
    span = ATT_TQ * max(d for _, d in ATT_PATTERNS)

    ffn_w_in, ffn_w_out, mem_w_kv, rec_w_in, rec_gate_w, rec_w_out, att_w_in, att_w_out = (
        w.astype(BF16) for w in (ffn_w_in, ffn_w_out, mem_w_kv, rec_w_in, rec_gate_w, rec_w_out,
                                 att_w_in, att_w_out))
    ffn_norm = ffn_norm[:, :, None, :]
    mix_norm = mix_norm[:, None, :]
    mem_norm = mem_norm[:, None, :]
    mem_qk_gain = mem_qk_gain[:, :, None, :]
    rec_conv_b = rec_conv_b[:, None, :]
    rec_gate_b = rec_gate_b.reshape(rec_gate_b.shape[0], 2, 2, W)
    rec_lambda = rec_lambda[:, :, None, :]
    att_qk_gain = att_qk_gain[:, :, :, None, :]

    x = x.reshape(T, D)
    for layer in range(depth):
        x = _ffn(x, ffn_norm, ffn_w_in, ffn_w_out, layer, 0, tm_ffn, tf)
        mk, mv = _memkv(mem, mem_norm, mem_w_kv, mem_qk_gain, layer)
        j = layer // 2
        if layer % 2 == 0:
            z = _inproj(x, mix_norm, rec_w_in, (layer, j), tm_proj, tn_proj, ts_rec)
            hb = _rec_scan(z, rec_conv_w, rec_conv_b, rec_gate_w, rec_gate_b, rec_lambda, j, 1, None, S, ts_rec)
            ya = _rec_scan(z, rec_conv_w, rec_conv_b, rec_gate_w, rec_gate_b, rec_lambda, j, 0, hb, S, ts_rec)
            ym = _memattn(z, 2 * W, mk, mv, mem_qk_gain, layer, S, tm_proj)
            x = _outproj(x, ya, ym, rec_w_out, j, tm_proj, tn_proj, ts_rec)
        else:
            zh = _inproj_heads(x, mix_norm, att_w_in, att_qk_gain, (layer, j), tm_proj, tn_proj)
            ya = _attention(zh, B, S, span)
            ym = _memattn(zh, ATT_GROUPS * 3 * ATT_WIDTH, mk, mv, mem_qk_gain, layer, S, tm_proj)
            x = _outproj(x, ya, ym, att_w_out, j, tm_proj, tn_proj)
        x = _ffn(x, ffn_norm, ffn_w_in, ffn_w_out, layer, 1, tm_ffn, tf)
    return x.reshape(B, S, D)
```

```python
import functools

import jax
import jax.numpy as jnp
import numpy as np
from jax import lax
from jax.experimental import pallas as pl
from jax.experimental.pallas import tpu as pltpu

F32 = jnp.float32
BF16 = jnp.bfloat16

EPS = 1e-6
NEG_INF = -1e30
LRU_C = 8.0
LRU_BLOCKS = 8
CONV_WIDTH = 4
CONV_LEFT = CONV_WIDTH // 2
ATT_PATTERNS = ((128, 1), (512, 4), (2048, 16))
ATT_GROUPS = len(ATT_PATTERNS)
ATT_HEADS = 8
ATT_HEAD_DIM = 128
ATT_WIDTH = ATT_HEADS * ATT_HEAD_DIM
MEM_HEADS = 4
SUBLANES = 8
LANES = 128
VMEM_LIMIT_BYTES = 56 * 1024 * 1024


def _params(*semantics):
    return pltpu.CompilerParams(dimension_semantics=semantics, vmem_limit_bytes=VMEM_LIMIT_BYTES)


def _rms(x, g):
    return x * lax.rsqrt(jnp.mean(x * x, axis=-1, keepdims=True) + EPS) * g


def _dot(a, b):
    return jnp.dot(a, b, preferred_element_type=F32)


def _dot_nt(a, b):
    return lax.dot_general(a, b, (((1,), (1,)), ((), ())), preferred_element_type=F32)


def _ffn_kernel(x_ref, g_ref, wg_ref, wu_ref, wo_ref, o_ref, xn_ref, acc_ref):
    j = pl.program_id(1)

    @pl.when(j == 0)
    def _():
        xn_ref[...] = _rms(x_ref[...], g_ref[...]).astype(BF16)
        acc_ref[...] = jnp.zeros_like(acc_ref)

    xn = xn_ref[...]
    gate = _dot(xn, wg_ref[...])
    up = _dot(xn, wu_ref[...])
    h = (jax.nn.silu(gate) * up).astype(BF16)
    acc_ref[...] += _dot(h, wo_ref[...])

    @pl.when(j == pl.num_programs(1) - 1)
    def _():
        o_ref[...] = x_ref[...] + 0.5 * acc_ref[...]


def _ffn(x, g, w_in, w_out, layer, k, tm, tf):
    T, D = x.shape
    F = w_out.shape[2]
    nf = F // tf
    return pl.pallas_call(
        _ffn_kernel,
        grid=(T // tm, nf),
        in_specs=[
            pl.BlockSpec((tm, D), lambda i, j: (i, 0)),
            pl.BlockSpec((None, None, 1, D), lambda i, j: (layer, k, 0, 0)),
            pl.BlockSpec((None, None, D, tf), lambda i, j: (layer, k, 0, j)),
            pl.BlockSpec((None, None, D, tf), lambda i, j: (layer, k, 0, j + nf)),
            pl.BlockSpec((None, None, tf, D), lambda i, j: (layer, k, j, 0)),
        ],
        out_specs=pl.BlockSpec((tm, D), lambda i, j: (i, 0)),
        out_shape=jax.ShapeDtypeStruct((T, D), F32),
        scratch_shapes=[pltpu.VMEM((tm, D), BF16), pltpu.VMEM((tm, D), F32)],
        compiler_params=_params("parallel", "arbitrary"),
        name="ffn",
    )(x, g, w_in, w_in, w_out)


def _segment_major_perm(chunk, inverse):
    seg = chunk // SUBLANES
    i = lax.broadcasted_iota(jnp.int32, (chunk, chunk), 0)
    j = lax.broadcasted_iota(jnp.int32, (chunk, chunk), 1)
    src = (i % seg) * SUBLANES + i // seg if inverse else (i % SUBLANES) * seg + i // SUBLANES
    return jnp.where(j == src, 1.0, 0.0).astype(BF16)


def _permute_chunks(dst_ref, src, chunk, inverse):
    perm = _segment_major_perm(chunk, inverse)
    for c in range(src.shape[0] // chunk):
        rows = slice(c * chunk, (c + 1) * chunk)
        dst_ref[rows, :] = _dot(perm, src[rows, :]).astype(BF16)


def _inproj_kernel(x_ref, g_ref, w_ref, o_ref, xn_ref, *, chunk):
    @pl.when(pl.program_id(1) == 0)
    def _():
        xn = _rms(x_ref[...], g_ref[...]).astype(BF16)
        _permute_chunks(xn_ref, xn, chunk, inverse=False)

    o_ref[...] = _dot(xn_ref[...], w_ref[...])


def _inproj(x, g, w, idx, tm, tn, chunk):
    T, D = x.shape
    N = w.shape[2]
    assert tm % chunk == 0
    return pl.pallas_call(
        functools.partial(_inproj_kernel, chunk=chunk),
        grid=(T // tm, N // tn),
        in_specs=[
            pl.BlockSpec((tm, D), lambda i, j: (i, 0)),
            pl.BlockSpec((None, 1, D), lambda i, j: (idx[0], 0, 0)),
            pl.BlockSpec((None, D, tn), lambda i, j: (idx[1], 0, j)),
        ],
        out_specs=pl.BlockSpec((tm, tn), lambda i, j: (i, j)),
        out_shape=jax.ShapeDtypeStruct((T, N), F32),
        scratch_shapes=[pltpu.VMEM((tm, D), BF16)],
        compiler_params=_params("parallel", "arbitrary"),
        name="inproj",
    )(x, g, w)


def _inproj_heads_kernel(x_ref, g_ref, w_ref, qkg_ref, o_ref, xn_ref, *, tiles_per_part):
    j = pl.program_id(1)

    @pl.when(j == 0)
    def _():
        xn_ref[...] = _rms(x_ref[...], g_ref[...]).astype(BF16)

    z = _dot(xn_ref[...], w_ref[...])
    part = j // tiles_per_part
    is_qk = (part < 3 * ATT_GROUPS) & (part % 3 < 2)
    gain = jnp.where(is_qk, qkg_ref[...], 1.0)
    for h in range(o_ref.shape[0]):
        zh = z[:, h * LANES:(h + 1) * LANES]
        inv = lax.rsqrt(jnp.mean(zh * zh, axis=-1, keepdims=True) + EPS)
        o_ref[h] = zh * jnp.where(is_qk, inv, 1.0) * gain


def _inproj_heads(x, g, w, qk_gain, idx, tm, tn):
    T, D = x.shape
    N = w.shape[2]
    assert ATT_HEAD_DIM == LANES and tn % LANES == 0 and ATT_WIDTH % tn == 0
    tpp = ATT_WIDTH // tn

    def gain_map(i, j):
        part = j // tpp
        return (idx[1], jnp.minimum(part % 3, 1), jnp.minimum(part // 3, ATT_GROUPS - 1), 0, 0)

    kern = functools.partial(_inproj_heads_kernel, tiles_per_part=tpp)
    return pl.pallas_call(
        kern,
        grid=(T // tm, N // tn),
        in_specs=[
            pl.BlockSpec((tm, D), lambda i, j: (i, 0)),
            pl.BlockSpec((None, 1, D), lambda i, j: (idx[0], 0, 0)),
            pl.BlockSpec((None, D, tn), lambda i, j: (idx[1], 0, j)),
            pl.BlockSpec((None, None, None, 1, LANES), gain_map),
        ],
        out_specs=pl.BlockSpec((tn // LANES, tm, LANES), lambda i, j: (j, i, 0)),
        out_shape=jax.ShapeDtypeStruct((N // LANES, T, LANES), F32),
        scratch_shapes=[pltpu.VMEM((tm, D), BF16)],
        compiler_params=_params("parallel", "arbitrary"),
        name="inproj_heads",
    )(x, g, w, qk_gain)


def _memkv_kernel(mem_ref, g_ref, w_ref, kg_ref, k_ref, v_ref, *, heads, hd):
    mem_n = _rms(mem_ref[...], g_ref[...]).astype(BF16)
    kv = _dot(mem_n, w_ref[...])
    for h in range(heads):
        sl = slice(h * hd, (h + 1) * hd)
        k_ref[:, sl] = _rms(kv[:, sl], kg_ref[...]).astype(BF16)
    v_ref[...] = kv[:, heads * hd:].astype(BF16)


def _memkv(mem, g, w_kv, qk_gain, layer):
    B, M, D = mem.shape
    mw = w_kv.shape[2] // 2
    hd = mw // MEM_HEADS
    kern = functools.partial(_memkv_kernel, heads=MEM_HEADS, hd=hd)
    return pl.pallas_call(
        kern,
        grid=(B,),
        in_specs=[
            pl.BlockSpec((None, M, D), lambda b: (b, 0, 0)),
            pl.BlockSpec((None, 1, D), lambda b: (layer, 0, 0)),
            pl.BlockSpec((None, D, 2 * mw), lambda b: (layer, 0, 0)),
            pl.BlockSpec((None, None, 1, hd), lambda b: (layer, 1, 0, 0)),
        ],
        out_specs=[pl.BlockSpec((None, M, mw), lambda b: (b, 0, 0))] * 2,
        out_shape=[jax.ShapeDtypeStruct((B, M, mw), BF16)] * 2,
        compiler_params=_params("parallel"),
        name="memkv",
    )(mem, g, w_kv, qk_gain)


def _memattn_kernel(q_ref, qg_ref, k_ref, v_ref, o_ref, *, heads, hd, head_major):
    scale = hd ** -0.5
    per = hd // LANES
    for h in range(heads):
        sl = slice(h * hd, (h + 1) * hd)
        if head_major:
            qh = jnp.concatenate([q_ref[h * per + c] for c in range(per)], axis=1)
        else:
            qh = q_ref[:, sl]
        q = _rms(qh, qg_ref[...]).astype(BF16)
        s = _dot_nt(q, k_ref[:, sl]) * scale
        e = jnp.exp(s - jnp.max(s, axis=-1, keepdims=True))
        p = e / jnp.sum(e, axis=-1, keepdims=True)
        o_ref[:, sl] = _dot(p.astype(BF16), v_ref[:, sl]).astype(o_ref.dtype)


def _memattn(z, q_col, mk, mv, qk_gain, layer, S, tm):
    head_major = z.ndim == 3
    T = z.shape[1] if head_major else z.shape[0]
    _, M, mw = mk.shape
    hd = mw // MEM_HEADS
    qb = q_col // mw
    assert qb * mw == q_col and hd % LANES == 0
    per_b = S // tm
    if head_major:
        q_spec = pl.BlockSpec((mw // LANES, tm, LANES), lambda i: (qb, i, 0))
    else:
        q_spec = pl.BlockSpec((tm, mw), lambda i: (i, qb))
    kern = functools.partial(_memattn_kernel, heads=MEM_HEADS, hd=hd, head_major=head_major)
    return pl.pallas_call(
        kern,
        grid=(T // tm,),
        in_specs=[
            q_spec,
            pl.BlockSpec((None, None, 1, hd), lambda i: (layer, 0, 0, 0)),
            pl.BlockSpec((None, M, mw), lambda i: (i // per_b, 0, 0)),
            pl.BlockSpec((None, M, mw), lambda i: (i // per_b, 0, 0)),
        ],
        out_specs=pl.BlockSpec((tm, mw), lambda i: (i, 0)),
        out_shape=jax.ShapeDtypeStruct((T, mw), BF16),
        compiler_params=_params("parallel"),
        name="memattn",
    )(z, qk_gain, mk, mv)


def _outproj_kernel(ya_ref, ym_ref, wa_ref, wm_ref, x_ref, o_ref, ya_seq, ym_seq, *, chunk):
    @pl.when(pl.program_id(1) == 0)
    def _():
        _permute_chunks(ya_seq, ya_ref[...], chunk, inverse=True)
        _permute_chunks(ym_seq, ym_ref[...], chunk, inverse=True)

    o_ref[...] = x_ref[...] + _dot(ya_seq[...], wa_ref[...]) + _dot(ym_seq[...], wm_ref[...])


def _outproj_heads_kernel(ya_ref, ym_ref, wa_ref, wm_ref, x_ref, o_ref, cat_ref):
    @pl.when(pl.program_id(1) == 0)
    def _():
        for h in range(ya_ref.shape[0]):
            cat_ref[:, h * LANES:(h + 1) * LANES] = ya_ref[h]

    o_ref[...] = x_ref[...] + _dot(cat_ref[...], wa_ref[...]) + _dot(ym_ref[...], wm_ref[...])


def _outproj(x, ya, ym, w, idx, tm, tn, chunk=None):
    T, D = x.shape
    head_major = ya.ndim == 3
    assert head_major == (chunk is None)
    ka = ya.shape[0] * LANES if head_major else ya.shape[1]
    km = ym.shape[1]
    mb = ka // km
    assert mb * km == ka
    if head_major:
        ya_spec = pl.BlockSpec((ka // LANES, tm, LANES), lambda i, j: (0, i, 0))
        kern, scratch = _outproj_heads_kernel, [pltpu.VMEM((tm, ka), BF16)]
    else:
        assert tm % chunk == 0
        ya_spec = pl.BlockSpec((tm, ka), lambda i, j: (i, 0))
        kern = functools.partial(_outproj_kernel, chunk=chunk)
        scratch = [pltpu.VMEM((tm, ka), BF16), pltpu.VMEM((tm, km), BF16)]
    return pl.pallas_call(
        kern,
        grid=(T // tm, D // tn),
        in_specs=[
            ya_spec,
            pl.BlockSpec((tm, km), lambda i, j: (i, 0)),
            pl.BlockSpec((None, ka, tn), lambda i, j: (idx, 0, j)),
            pl.BlockSpec((None, km, tn), lambda i, j: (idx, mb, j)),
            pl.BlockSpec((tm, tn), lambda i, j: (i, j)),
        ],
        out_specs=pl.BlockSpec((tm, tn), lambda i, j: (i, j)),
        out_shape=jax.ShapeDtypeStruct((T, D), F32),
        scratch_shapes=scratch,
        compiler_params=_params("parallel", "arbitrary"),
        name="outproj",
    )(ya, ym, w, w, x)


def _rec_kernel(*refs, reverse, final, ts, bw):
    if final:
        (prev_ref, main_ref, next_ref, cw_ref, cb_ref, gw_ref, gb_ref, lam_ref,
         gate_ref, hb_ref, o_ref, a_ref, u_ref, h_ref) = refs
    else:
        (prev_ref, main_ref, next_ref, cw_ref, cb_ref, gw_ref, gb_ref, lam_ref,
         o_ref, a_ref, u_ref, h_ref) = refs
    c = pl.program_id(1)
    nc = pl.num_programs(1)
    chunk = nc - 1 - c if reverse else c
    seg = ts // SUBLANES
    W = main_ref.shape[1]
    assert CONV_WIDTH == 4 and CONV_LEFT == 2

    @pl.when(c == 0)
    def _():
        h_ref[...] = jnp.zeros_like(h_ref)

    main = main_ref[...]
    prev = jnp.where(chunk > 0, prev_ref[...], 0.0)
    nxt = jnp.where(chunk < nc - 1, next_ref[...], 0.0)
    sub = lax.broadcasted_iota(jnp.int32, (SUBLANES, W), 0)

    def from_prev_segment(tile, fill):
        return jnp.where(sub == 0, fill, pltpu.roll(tile, 1, axis=0))

    def from_next_segment(tile, fill):
        return jnp.where(sub == SUBLANES - 1, fill, pltpu.roll(tile, SUBLANES - 1, axis=0))

    back1 = from_prev_segment(main[ts - SUBLANES:], prev[2 * SUBLANES - 1:])
    back2 = from_prev_segment(main[ts - 2 * SUBLANES:ts - SUBLANES], prev[SUBLANES - 1:SUBLANES])
    ahead1 = from_next_segment(main[:SUBLANES], nxt[0:1])
    taps = (jnp.concatenate([back2, back1, main[:ts - 2 * SUBLANES]], axis=0),
            jnp.concatenate([back1, main[:ts - SUBLANES]], axis=0),
            main,
            jnp.concatenate([main[SUBLANES:], ahead1], axis=0))
    xc = cb_ref[...]
    for k in range(CONV_WIDTH):
        xc = xc + cw_ref[k:k + 1, :] * taps[k]

    log_a_scale = (-0.5 * LRU_C) * jax.nn.softplus(-lam_ref[...])
    xh = 0.5 * xc
    xhb = xh.astype(BF16)
    gbh = 0.5 * gb_ref[...]
    for n in range(LRU_BLOCKS):
        sl = slice(n * bw, (n + 1) * bw)
        tr = jnp.tanh(_dot(xhb[:, sl], gw_ref[0, n]) + gbh[0:1, sl])
        ti = jnp.tanh(_dot(xhb[:, sl], gw_ref[1, n]) + gbh[1:2, sl])
        a = jnp.exp((tr + 1.0) * log_a_scale[:, sl])
        y = 1.0 - a * a
        a_ref[:, sl] = a
        u_ref[:, sl] = (y * lax.rsqrt(jnp.maximum(y, 1e-30))) * ((ti + 1.0) * xh[:, sl])

    def step(t, carry):
        h, prod = carry
        k = seg - 1 - t if reverse else t
        rows = pl.ds(pl.multiple_of(k * SUBLANES, SUBLANES), SUBLANES)
        a = a_ref[rows, :]
        h = a * h + u_ref[rows, :]
        prod = a * prod
        u_ref[rows, :] = h
        a_ref[rows, :] = prod
        return h, prod

    h_end, a_end = lax.fori_loop(0, seg, step, (jnp.zeros((SUBLANES, W), F32), jnp.ones((SUBLANES, W), F32)),
                                 unroll=2)

    carry = h_ref[...]
    h_in = jnp.zeros((SUBLANES, W), F32)
    for s in (range(SUBLANES - 1, -1, -1) if reverse else range(SUBLANES)):
        h_in = jnp.where(sub == s, carry, h_in)
        carry = h_end[s:s + 1] + a_end[s:s + 1] * carry
    h_ref[...] = carry

    hs = u_ref[...].reshape(seg, SUBLANES, W) + a_ref[...].reshape(seg, SUBLANES, W) * h_in[None]
    hs = hs.reshape(ts, W)
    if final:
        o_ref[...] = ((hs + hb_ref[...]) * jax.nn.gelu(gate_ref[...])).astype(o_ref.dtype)
    else:
        o_ref[...] = hs


def _rec_scan(z, conv_w, conv_b, gate_w, gate_b, lam, j, direction, h_other, S, ts):
    T = z.shape[0]
    W = conv_w.shape[2]
    bw = W // LRU_BLOCKS
    nc = S // ts
    reverse = direction == 1
    final = h_other is not None
    assert ts % (2 * SUBLANES) == 0

    def chunk_of(c):
        return nc - 1 - c if reverse else c

    def main_map(b, c):
        return (b * nc + chunk_of(c), 1)

    def prev_map(b, c):
        return (jnp.maximum((b * nc + chunk_of(c)) * (ts // (2 * SUBLANES)) - 1, 0), 1)

    def next_map(b, c):
        return (jnp.minimum((b * nc + chunk_of(c) + 1) * (ts // SUBLANES), T // SUBLANES - 1), 1)

    in_specs = [
        pl.BlockSpec((2 * SUBLANES, W), prev_map),
        pl.BlockSpec((ts, W), main_map),
        pl.BlockSpec((SUBLANES, W), next_map),
        pl.BlockSpec((None, CONV_WIDTH, W), lambda b, c: (j, 0, 0)),
        pl.BlockSpec((None, 1, W), lambda b, c: (j, 0, 0)),
        pl.BlockSpec((None, None, 2, LRU_BLOCKS, bw, bw), lambda b, c: (j, direction, 0, 0, 0, 0)),
        pl.BlockSpec((None, None, 2, W), lambda b, c: (j, direction, 0, 0)),
        pl.BlockSpec((None, None, 1, W), lambda b, c: (j, direction, 0, 0)),
    ]
    args = [z, z, z, conv_w, conv_b, gate_w, gate_b, lam]
    if final:
        in_specs += [
            pl.BlockSpec((ts, W), lambda b, c: (b * nc + chunk_of(c), 0)),
            pl.BlockSpec((ts, W), lambda b, c: (b * nc + chunk_of(c), 0)),
        ]
        args += [z, h_other]
    kern = functools.partial(_rec_kernel, reverse=reverse, final=final, ts=ts, bw=bw)
    return pl.pallas_call(
        kern,
        grid=(T // S, nc),
        in_specs=in_specs,
        out_specs=pl.BlockSpec((ts, W), lambda b, c: (b * nc + chunk_of(c), 0)),
        out_shape=jax.ShapeDtypeStruct((T, W), BF16 if final else F32),
        scratch_shapes=[
            pltpu.VMEM((ts, W), F32),
            pltpu.VMEM((ts, W), F32),
            pltpu.VMEM((1, W), F32),
        ],
        compiler_params=_params("parallel", "arbitrary"),
        name="rec_fwd" if final else "rec_bwd",
    )(*args)


ATT_TQ = 128
ATT_INTERLEAVE = 16


def _alibi_slopes():
    n = ATT_GROUPS * ATT_HEADS
    s = [2.0 ** (-8.0 * (i + 1.0) / n) for i in range(n)]
    return np.asarray(s, np.float32).reshape(ATT_GROUPS, ATT_HEADS)


def _rows(start, size, stride):
    if stride > 1:
        return pl.ds(start, size, stride=stride)
    return pl.ds(start if isinstance(start, int) else pl.multiple_of(start, SUBLANES), size)


def _attn_kernel(slopes_ref, *refs, span, halves):
    o_ref, acc_o, acc_l = refs[7 * ATT_GROUPS:]
    head = pl.program_id(1)
    blk = pl.program_id(2)
    nblk = pl.num_programs(2)
    tq, hd = ATT_TQ, ATT_HEAD_DIM
    half = halves[0]
    nk = tq + 2 * half
    row = lax.broadcasted_iota(jnp.int32, (tq, nk), 0)
    col = lax.broadcasted_iota(jnp.int32, (tq, nk), 1)
    rel = jnp.abs(col - half - row)
    in_band = rel <= half
    rel_f = rel.astype(F32)
    col1 = lax.broadcasted_iota(jnp.int32, (1, nk), 1)
    lo_mask = jnp.where((col1 < half) & (blk == 0), NEG_INF, 0.0)
    hi_mask = jnp.where((col1 >= nk - half) & (blk == nblk - 1), NEG_INF, 0.0)
    scale = hd ** -0.5

    for g, (_, d) in enumerate(ATT_PATTERNS):
        assert halves[g] == half
        q_ref, kp_ref, kc_ref, kn_ref, vp_ref, vc_ref, vn_ref = refs[7 * g:7 * g + 7]
        nu = span // (tq * d)
        slope = slopes_ref[g, head]
        bias_mid = jnp.where(in_band, -slope * (d * rel_f), NEG_INF)
        biases = {"mid": bias_mid, "first": bias_mid + lo_mask, "last": bias_mid + hi_mask,
                  "only": bias_mid + lo_mask + hi_mask}

        def tiles(specs, g=g, d=d, q_ref=q_ref, kp_ref=kp_ref, kc_ref=kc_ref, kn_ref=kn_ref,
                  vp_ref=vp_ref, vc_ref=vc_ref, vn_ref=vn_ref, biases=biases):
            done = []
            for r, u, kind in specs:
                qrows = _rows(u * (tq * d) + r, tq, d)
                lo = (u * tq - half) * d + r
                if kind == "only":
                    pieces = [(kp_ref, vp_ref, _rows(r, half, d)), (kc_ref, vc_ref, _rows(r, tq, d)),
                              (kn_ref, vn_ref, _rows(r, half, d))]
                elif kind == "first":
                    pieces = [(kp_ref, vp_ref, _rows(r, half, d)), (kc_ref, vc_ref, _rows(r, tq + half, d))]
                elif kind == "last":
                    pieces = [(kc_ref, vc_ref, _rows(lo, tq + half, d)), (kn_ref, vn_ref, _rows(r, half, d))]
                else:
                    pieces = [(kc_ref, vc_ref, _rows(lo, nk, d))]
                k = jnp.concatenate([kr[rows, :] for kr, _, rows in pieces], axis=0).astype(BF16)
                v = jnp.concatenate([vr[rows, :] for _, vr, rows in pieces], axis=0).astype(BF16)
                q = q_ref[qrows, :].astype(BF16)
                s = _dot_nt(q, k) * scale + biases[kind]
                m = jnp.max(s, axis=-1, keepdims=True)
                e = jnp.exp(s - m)
                den = jnp.sum(e, axis=-1, keepdims=True)
                o = _dot((e / den).astype(BF16), v)
                lse = jnp.broadcast_to(m + jnp.log(den), (tq, hd))
                prev = (acc_o[qrows, :], acc_l[qrows, :]) if g > 0 else None
                done.append((qrows, o, lse, prev))
            for qrows, o, lse, prev in done:
                if prev is not None:
                    o_acc, lse_acc = prev
                    top = jnp.maximum(lse_acc, lse)
                    w_acc = jnp.exp(lse_acc - top)
                    w_cur = jnp.exp(lse - top)
                    tot = w_acc + w_cur
                    o = (w_acc * o_acc + w_cur * o) / tot
                    lse = top + jnp.log(tot)
                acc_o[qrows, :] = o
                if g < ATT_GROUPS - 1:
                    acc_l[qrows, :] = lse

        def kind_of(u, nu=nu):
            return "only" if nu == 1 else "first" if u == 0 else "last" if u == nu - 1 else "mid"

        ilv = ATT_INTERLEAVE
        if nu >= ilv:
            assert nu % ilv == 0
            nchunk = nu // ilv

            def per_class(r, nu=nu, nchunk=nchunk, tiles=tiles, kind_of=kind_of):
                tiles([(r, u, kind_of(u)) for u in range(ilv)])
                if nchunk > 2:
                    def mid(c, carry):
                        tiles([(r, c * ilv + t, "mid") for t in range(ilv)])
                        return carry
                    lax.fori_loop(1, nchunk - 1, mid, 0)
                if nchunk > 1:
                    tiles([(r, u, kind_of(u)) for u in range(nu - ilv, nu)])

            if d == 1:
                per_class(0)
            else:
                def classes(r, carry, per_class=per_class):
                    per_class(r)
                    return carry
                lax.fori_loop(0, d, classes, 0)
        else:
            per_blk = ilv // nu
            assert per_blk * nu == ilv and d % per_blk == 0

            def classes(rb, carry, nu=nu, per_blk=per_blk, tiles=tiles, kind_of=kind_of):
                tiles([(rb * per_blk + t, u, kind_of(u)) for t in range(per_blk) for u in range(nu)])
                return carry
            lax.fori_loop(0, d // per_blk, classes, 0)

    o_ref[...] = acc_o[...].astype(o_ref.dtype)


def _attention(zh, B, S, span):
    T = zh.shape[1]
    nb = S // span
    halves = tuple(w // (2 * d) for w, d in ATT_PATTERNS)
    in_specs = [pl.BlockSpec(memory_space=pltpu.SMEM)]
    for g, (_, d) in enumerate(ATT_PATTERNS):
        halo = halves[g] * d
        assert span % (ATT_TQ * d) == 0 and span % halo == 0 and S % span == 0
        ratio = span // halo

        def slab(p, g=g):
            return (3 * g + p) * ATT_HEADS

        def cur(p, slab=slab):
            return pl.BlockSpec((None, span, LANES), lambda b, h, i: (slab(p) + h, b * nb + i, 0))

        def before(p, slab=slab, halo=halo, ratio=ratio):
            return pl.BlockSpec((None, halo, LANES),
                                lambda b, h, i: (slab(p) + h, jnp.maximum((b * nb + i) * ratio - 1, 0), 0))

        def after(p, slab=slab, halo=halo, ratio=ratio):
            return pl.BlockSpec((None, halo, LANES),
                                lambda b, h, i: (slab(p) + h, jnp.minimum((b * nb + i + 1) * ratio, T // halo - 1), 0))

        in_specs += [cur(0), before(1), cur(1), after(1), before(2), cur(2), after(2)]
    kern = functools.partial(_attn_kernel, span=span, halves=halves)
    return pl.pallas_call(
        kern,
        grid=(B, ATT_HEADS, nb),
        in_specs=in_specs,
        out_specs=pl.BlockSpec((None, span, LANES), lambda b, h, i: (h, b * nb + i, 0)),
        out_shape=jax.ShapeDtypeStruct((ATT_HEADS, T, LANES), BF16),
        scratch_shapes=[pltpu.VMEM((span, LANES), F32), pltpu.VMEM((span, LANES), F32)],
        compiler_params=_params("parallel", "parallel", "arbitrary"),
        name="attn",
    )(jnp.asarray(_alibi_slopes()), *([zh] * (7 * ATT_GROUPS)))


def kernel(x, mem, ffn_norm, ffn_w_in, ffn_w_out, mix_norm, mem_norm, mem_w_kv, mem_qk_gain,
           rec_w_in, rec_conv_w, rec_conv_b, rec_gate_w, rec_gate_b, rec_lambda, rec_w_out,
           att_w_in, att_qk_gain, att_w_out):
    B, S, D = x.shape
    T = B * S
    depth = ffn_norm.shape[0]
    W = rec_conv_w.shape[2]

    tm_ffn = min(512, S)
    tf = min(512, ffn_w_out.shape[2])
    tm_proj = min(1024, S)
    tn_proj = min(1024, D)
    ts_rec = min(256, S)
    span = ATT_TQ * max(d for _, d in ATT_PATTERNS)

    ffn_w_in, ffn_w_out, mem_w_kv, rec_w_in, rec_gate_w, rec_w_out, att_w_in, att_w_out = (
        w.astype(BF16) for w in (ffn_w_in, ffn_w_out, mem_w_kv, rec_w_in, rec_gate_w, rec_w_out,
                                 att_w_in, att_w_out))
    ffn_norm = ffn_norm[:, :, None, :]
    mix_norm = mix_norm[:, None, :]
    mem_norm = mem_norm[:, None, :]
    mem_qk_gain = mem_qk_gain[:, :, None, :]
    rec_conv_b = rec_conv_b[:, None, :]
    rec_gate_b = rec_gate_b.reshape(rec_gate_b.shape[0], 2, 2, W)
    rec_lambda = rec_lambda[:, :, None, :]
    att_qk_gain = att_qk_gain[:, :, :, None, :]

    x = x.reshape(T, D)
    for layer in range(depth):
        x = _ffn(x, ffn_norm, ffn_w_in, ffn_w_out, layer, 0, tm_ffn, tf)
        mk, mv = _memkv(mem, mem_norm, mem_w_kv, mem_qk_gain, layer)
        j = layer // 2
        if layer % 2 == 0:
            z = _inproj(x, mix_norm, rec_w_in, (layer, j), tm_proj, tn_proj, ts_rec)
            hb = _rec_scan(z, rec_conv_w, rec_conv_b, rec_gate_w, rec_gate_b, rec_lambda, j, 1, None, S, ts_rec)
            ya = _rec_scan(z, rec_conv_w, rec_conv_b, rec_gate_w, rec_gate_b, rec_lambda, j, 0, hb, S, ts_rec)
            ym = _memattn(z, 2 * W, mk, mv, mem_qk_gain, layer, S, tm_proj)
            x = _outproj(x, ya, ym, rec_w_out, j, tm_proj, tn_proj, ts_rec)
        else:
            zh = _inproj_heads(x, mix_norm, att_w_in, att_qk_gain, (layer, j), tm_proj, tn_proj)
            ya = _attention(zh, B, S, span)
            ym = _memattn(zh, ATT_GROUPS * 3 * ATT_WIDTH, mk, mv, mem_qk_gain, layer, S, tm_proj)
            x = _outproj(x, ya, ym, att_w_out, j, tm_proj, tn_proj)
        x = _ffn(x, ffn_norm, ffn_w_in, ffn_w_out, layer, 1, tm_ffn, tf)
    return x.reshape(B, S, D)
```

```python
import functools

import jax
import jax.numpy as jnp
import numpy as np
from jax import lax
from jax.experimental import pallas as pl
from jax.experimental.pallas import tpu as pltpu

F32 = jnp.float32
BF16 = jnp.bfloat16

EPS = 1e-6
NEG_INF = -1e30
LRU_C = 8.0
LRU_BLOCKS = 8
CONV_WIDTH = 4
CONV_LEFT = CONV_WIDTH // 2
ATT_PATTERNS = ((128, 1), (512, 4), (2048, 16))
ATT_GROUPS = len(ATT_PATTERNS)
ATT_HEADS = 8
ATT_HEAD_DIM = 128
ATT_WIDTH = ATT_HEADS * ATT_HEAD_DIM
MEM_HEADS = 4
SUBLANES = 8
LANES = 128
VMEM_LIMIT_BYTES = 56 * 1024 * 1024


def _params(*semantics):
    return pltpu.CompilerParams(dimension_semantics=semantics, vmem_limit_bytes=VMEM_LIMIT_BYTES)


def _rms(x, g):
    return x * lax.rsqrt(jnp.mean(x * x, axis=-1, keepdims=True) + EPS) * g


def _dot(a, b):
    return jnp.dot(a, b, preferred_element_type=F32)


def _dot_nt(a, b):
    return lax.dot_general(a, b, (((1,), (1,)), ((), ())), preferred_element_type=F32)


def _ffn_kernel(x_ref, g_ref, wg_ref, wu_ref, wo_ref, o_ref, xn_ref):
    j = pl.program_id(1)

    @pl.when(j == 0)
    def _():
        xn_ref[...] = _rms(x_ref[...], g_ref[...]).astype(BF16)
        o_ref[...] = jnp.zeros_like(o_ref)

    xn = xn_ref[...]
    gate = _dot(xn, wg_ref[...])
    up = _dot(xn, wu_ref[...])
    h = (jax.nn.silu(gate) * up).astype(BF16)
    o_ref[...] += _dot(h, wo_ref[...])

    @pl.when(j == pl.num_programs(1) - 1)
    def _():
        o_ref[...] = x_ref[...] + 0.5 * o_ref[...]


def _ffn(x, g, w_in, w_out, layer, k, tm, tf):
    T, D = x.shape
    F = w_out.shape[2]
    nf = F // tf
    return pl.pallas_call(
        _ffn_kernel,
        grid=(T // tm, nf),
        in_specs=[
            pl.BlockSpec((tm, D), lambda i, j: (i, 0)),
            pl.BlockSpec((None, None, 1, D), lambda i, j: (layer, k, 0, 0)),
            pl.BlockSpec((None, None, D, tf), lambda i, j: (layer, k, 0, j)),
            pl.BlockSpec((None, None, D, tf), lambda i, j: (layer, k, 0, j + nf)),
            pl.BlockSpec((None, None, tf, D), lambda i, j: (layer, k, j, 0)),
        ],
        out_specs=pl.BlockSpec((tm, D), lambda i, j: (i, 0)),
        out_shape=jax.ShapeDtypeStruct((T, D), F32),
        scratch_shapes=[pltpu.VMEM((tm, D), BF16)],
        compiler_params=_params("parallel", "arbitrary"),
        name="ffn",
    )(x, g, w_in, w_in, w_out)


def _segment_major_perm(chunk, inverse):
    seg = chunk // SUBLANES
    i = lax.broadcasted_iota(jnp.int32, (chunk, chunk), 0)
    j = lax.broadcasted_iota(jnp.int32, (chunk, chunk), 1)
    src = (i % seg) * SUBLANES + i // seg if inverse else (i % SUBLANES) * seg + i // SUBLANES
    return jnp.where(j == src, 1.0, 0.0).astype(BF16)


def _permute_chunks(dst_ref, src, chunk, inverse):
    perm = _segment_major_perm(chunk, inverse)
    for c in range(src.shape[0] // chunk):
        rows = slice(c * chunk, (c + 1) * chunk)
        dst_ref[rows, :] = _dot(perm, src[rows, :]).astype(BF16)


def _inproj_kernel(x_ref, g_ref, w_ref, o_ref, xn_ref, *, chunk):
    @pl.when(pl.program_id(1) == 0)
    def _():
        xn = _rms(x_ref[...], g_ref[...]).astype(BF16)
        _permute_chunks(xn_ref, xn, chunk, inverse=False)

    o_ref[...] = _dot(xn_ref[...], w_ref[...])


def _inproj(x, g, w, idx, tm, tn, chunk):
    T, D = x.shape
    N = w.shape[2]
    assert tm % chunk == 0
    return pl.pallas_call(
        functools.partial(_inproj_kernel, chunk=chunk),
        grid=(T // tm, N // tn),
        in_specs=[
            pl.BlockSpec((tm, D), lambda i, j: (i, 0)),
            pl.BlockSpec((None, 1, D), lambda i, j: (idx[0], 0, 0)),
            pl.BlockSpec((None, D, tn), lambda i, j: (idx[1], 0, j)),
        ],
        out_specs=pl.BlockSpec((tm, tn), lambda i, j: (i, j)),
        out_shape=jax.ShapeDtypeStruct((T, N), F32),
        scratch_shapes=[pltpu.VMEM((tm, D), BF16)],
        compiler_params=_params("parallel", "arbitrary"),
        name="inproj",
    )(x, g, w)


def _inproj_heads_kernel(x_ref, g_ref, w_ref, qkg_ref, o_ref, xn_ref, *, tiles_per_part):
    j = pl.program_id(1)

    @pl.when(j == 0)
    def _():
        xn_ref[...] = _rms(x_ref[...], g_ref[...]).astype(BF16)

    z = _dot(xn_ref[...], w_ref[...])
    part = j // tiles_per_part
    is_qk = (part < 3 * ATT_GROUPS) & (part % 3 < 2)
    gain = jnp.where(is_qk, qkg_ref[...], 1.0)
    for h in range(o_ref.shape[0]):
        zh = z[:, h * LANES:(h + 1) * LANES]
        inv = lax.rsqrt(jnp.mean(zh * zh, axis=-1, keepdims=True) + EPS)
        o_ref[h] = zh * jnp.where(is_qk, inv, 1.0) * gain


def _inproj_heads(x, g, w, qk_gain, idx, tm, tn):
    T, D = x.shape
    N = w.shape[2]
    assert ATT_HEAD_DIM == LANES and tn % LANES == 0 and ATT_WIDTH % tn == 0
    tpp = ATT_WIDTH // tn

    def gain_map(i, j):
        part = j // tpp
        return (idx[1], jnp.minimum(part % 3, 1), jnp.minimum(part // 3, ATT_GROUPS - 1), 0, 0)

    kern = functools.partial(_inproj_heads_kernel, tiles_per_part=tpp)
    return pl.pallas_call(
        kern,
        grid=(T // tm, N // tn),
        in_specs=[
            pl.BlockSpec((tm, D), lambda i, j: (i, 0)),
            pl.BlockSpec((None, 1, D), lambda i, j: (idx[0], 0, 0)),
            pl.BlockSpec((None, D, tn), lambda i, j: (idx[1], 0, j)),
            pl.BlockSpec((None, None, None, 1, LANES), gain_map),
        ],
        out_specs=pl.BlockSpec((tn // LANES, tm, LANES), lambda i, j: (j, i, 0)),
        out_shape=jax.ShapeDtypeStruct((N // LANES, T, LANES), F32),
        scratch_shapes=[pltpu.VMEM((tm, D), BF16)],
        compiler_params=_params("parallel", "arbitrary"),
        name="inproj_heads",
    )(x, g, w, qk_gain)


def _memkv_kernel(mem_ref, g_ref, w_ref, kg_ref, k_ref, v_ref, *, heads, hd):
    mem_n = _rms(mem_ref[...], g_ref[...]).astype(BF16)
    kv = _dot(mem_n, w_ref[...])
    for h in range(heads):
        sl = slice(h * hd, (h + 1) * hd)
        k_ref[:, sl] = _rms(kv[:, sl], kg_ref[...]).astype(BF16)
    v_ref[...] = kv[:, heads * hd:].astype(BF16)


def _memkv(mem, g, w_kv, qk_gain, layer):
    B, M, D = mem.shape
    mw = w_kv.shape[2] // 2
    hd = mw // MEM_HEADS
    kern = functools.partial(_memkv_kernel, heads=MEM_HEADS, hd=hd)
    return pl.pallas_call(
        kern,
        grid=(B,),
        in_specs=[
            pl.BlockSpec((None, M, D), lambda b: (b, 0, 0)),
            pl.BlockSpec((None, 1, D), lambda b: (layer, 0, 0)),
            pl.BlockSpec((None, D, 2 * mw), lambda b: (layer, 0, 0)),
            pl.BlockSpec((None, None, 1, hd), lambda b: (layer, 1, 0, 0)),
        ],
        out_specs=[pl.BlockSpec((None, M, mw), lambda b: (b, 0, 0))] * 2,
        out_shape=[jax.ShapeDtypeStruct((B, M, mw), BF16)] * 2,
        compiler_params=_params("parallel"),
        name="memkv",
    )(mem, g, w_kv, qk_gain)


def _memattn_kernel(q_ref, qg_ref, k_ref, v_ref, o_ref, *, heads, hd, head_major):
    scale = hd ** -0.5
    per = hd // LANES
    for h in range(heads):
        sl = slice(h * hd, (h + 1) * hd)
        if head_major:
            qh = jnp.concatenate([q_ref[h * per + c] for c in range(per)], axis=1)
        else:
            qh = q_ref[:, sl]
        q = _rms(qh, qg_ref[...]).astype(BF16)
        s = _dot_nt(q, k_ref[:, sl]) * scale
        e = jnp.exp(s - jnp.max(s, axis=-1, keepdims=True))
        p = e / jnp.sum(e, axis=-1, keepdims=True)
        o_ref[:, sl] = _dot(p.astype(BF16), v_ref[:, sl]).astype(o_ref.dtype)


def _memattn(z, q_col, mk, mv, qk_gain, layer, S, tm):
    head_major = z.ndim == 3
    T = z.shape[1] if head_major else z.shape[0]
    _, M, mw = mk.shape
    hd = mw // MEM_HEADS
    qb = q_col // mw
    assert qb * mw == q_col and hd % LANES == 0
    per_b = S // tm
    if head_major:
        q_spec = pl.BlockSpec((mw // LANES, tm, LANES), lambda i: (qb, i, 0))
    else:
        q_spec = pl.BlockSpec((tm, mw), lambda i: (i, qb))
    kern = functools.partial(_memattn_kernel, heads=MEM_HEADS, hd=hd, head_major=head_major)
    return pl.pallas_call(
        kern,
        grid=(T // tm,),
        in_specs=[
            q_spec,
            pl.BlockSpec((None, None, 1, hd), lambda i: (layer, 0, 0, 0)),
            pl.BlockSpec((None, M, mw), lambda i: (i // per_b, 0, 0)),
            pl.BlockSpec((None, M, mw), lambda i: (i // per_b, 0, 0)),
        ],
        out_specs=pl.BlockSpec((tm, mw), lambda i: (i, 0)),
        out_shape=jax.ShapeDtypeStruct((T, mw), BF16),
        compiler_params=_params("parallel"),
        name="memattn",
    )(z, qk_gain, mk, mv)


def _outproj_kernel(ya_ref, ym_ref, wa_ref, wm_ref, x_ref, o_ref, ya_seq, ym_seq, *, chunk):
    @pl.when(pl.program_id(1) == 0)
    def _():
        _permute_chunks(ya_seq, ya_ref[...], chunk, inverse=True)
        _permute_chunks(ym_seq, ym_ref[...], chunk, inverse=True)

    o_ref[...] = x_ref[...] + _dot(ya_seq[...], wa_ref[...]) + _dot(ym_seq[...], wm_ref[...])


def _outproj_heads_kernel(ya_ref, ym_ref, wa_ref, wm_ref, x_ref, o_ref, cat_ref):
    @pl.when(pl.program_id(1) == 0)
    def _():
        for h in range(ya_ref.shape[0]):
            cat_ref[:, h * LANES:(h + 1) * LANES] = ya_ref[h]

    o_ref[...] = x_ref[...] + _dot(cat_ref[...], wa_ref[...]) + _dot(ym_ref[...], wm_ref[...])


def _outproj(x, ya, ym, w, idx, tm, tn, chunk=None):
    T, D = x.shape
    head_major = ya.ndim == 3
    assert head_major == (chunk is None)
    ka = ya.shape[0] * LANES if head_major else ya.shape[1]
    km = ym.shape[1]
    mb = ka // km
    assert mb * km == ka
    if head_major:
        ya_spec = pl.BlockSpec((ka // LANES, tm, LANES), lambda i, j: (0, i, 0))
        kern, scratch = _outproj_heads_kernel, [pltpu.VMEM((tm, ka), BF16)]
    else:
        assert tm % chunk == 0
        ya_spec = pl.BlockSpec((tm, ka), lambda i, j: (i, 0))
        kern = functools.partial(_outproj_kernel, chunk=chunk)
        scratch = [pltpu.VMEM((tm, ka), BF16), pltpu.VMEM((tm, km), BF16)]
    return pl.pallas_call(
        kern,
        grid=(T // tm, D // tn),
        in_specs=[
            ya_spec,
            pl.BlockSpec((tm, km), lambda i, j: (i, 0)),
            pl.BlockSpec((None, ka, tn), lambda i, j: (idx, 0, j)),
            pl.BlockSpec((None, km, tn), lambda i, j: (idx, mb, j)),
            pl.BlockSpec((tm, tn), lambda i, j: (i, j)),
        ],
        out_specs=pl.BlockSpec((tm, tn), lambda i, j: (i, j)),
        out_shape=jax.ShapeDtypeStruct((T, D), F32),
        scratch_shapes=scratch,
        compiler_params=_params("parallel", "arbitrary"),
        name="outproj",
    )(ya, ym, w, w, x)


def _rec_kernel(*refs, reverse, final, ts, bw):
    if final:
        (prev_ref, main_ref, next_ref, cw_ref, cb_ref, gw_ref, gb_ref, lam_ref,
         gate_ref, hb_ref, o_ref, a_ref, u_ref, h_ref) = refs
    else:
        (prev_ref, main_ref, next_ref, cw_ref, cb_ref, gw_ref, gb_ref, lam_ref,
         o_ref, a_ref, u_ref, h_ref) = refs
    c = pl.program_id(1)
    nc = pl.num_programs(1)
    chunk = nc - 1 - c if reverse else c
    seg = ts // SUBLANES
    W = main_ref.shape[1]
    assert CONV_WIDTH == 4 and CONV_LEFT == 2

    @pl.when(c == 0)
    def _():
        h_ref[...] = jnp.zeros_like(h_ref)

    main = main_ref[...]
    prev = jnp.where(chunk > 0, prev_ref[...], 0.0)
    nxt = jnp.where(chunk < nc - 1, next_ref[...], 0.0)
    sub = lax.broadcasted_iota(jnp.int32, (SUBLANES, W), 0)

    def from_prev_segment(tile, fill):
        return jnp.where(sub == 0, fill, pltpu.roll(tile, 1, axis=0))

    def from_next_segment(tile, fill):
        return jnp.where(sub == SUBLANES - 1, fill, pltpu.roll(tile, SUBLANES - 1, axis=0))

    back1 = from_prev_segment(main[ts - SUBLANES:], prev[2 * SUBLANES - 1:])
    back2 = from_prev_segment(main[ts - 2 * SUBLANES:ts - SUBLANES], prev[SUBLANES - 1:SUBLANES])
    ahead1 = from_next_segment(main[:SUBLANES], nxt[0:1])
    taps = (jnp.concatenate([back2, back1, main[:ts - 2 * SUBLANES]], axis=0),
            jnp.concatenate([back1, main[:ts - SUBLANES]], axis=0),
            main,
            jnp.concatenate([main[SUBLANES:], ahead1], axis=0))
    xc = cb_ref[...]
    for k in range(CONV_WIDTH):
        xc = xc + cw_ref[k:k + 1, :] * taps[k]

    log_a_scale = (-0.5 * LRU_C) * jax.nn.softplus(-lam_ref[...])
    xh = 0.5 * xc
    xhb = xh.astype(BF16)
    gbh = 0.5 * gb_ref[...]
    for n in range(LRU_BLOCKS):
        sl = slice(n * bw, (n + 1) * bw)
        tr = jnp.tanh(_dot(xhb[:, sl], gw_ref[0, n]) + gbh[0:1, sl])
        ti = jnp.tanh(_dot(xhb[:, sl], gw_ref[1, n]) + gbh[1:2, sl])
        a = jnp.exp((tr + 1.0) * log_a_scale[:, sl])
        y = 1.0 - a * a
        a_ref[:, sl] = a
        u_ref[:, sl] = (y * lax.rsqrt(jnp.maximum(y, 1e-30))) * ((ti + 1.0) * xh[:, sl])

    def step(t, carry):
        h, prod = carry
        k = seg - 1 - t if reverse else t
        rows = pl.ds(pl.multiple_of(k * SUBLANES, SUBLANES), SUBLANES)
        a = a_ref[rows, :]
        h = a * h + u_ref[rows, :]
        prod = a * prod
        u_ref[rows, :] = h
        a_ref[rows, :] = prod
        return h, prod

    h_end, a_end = lax.fori_loop(0, seg, step, (jnp.zeros((SUBLANES, W), F32), jnp.ones((SUBLANES, W), F32)),
                                 unroll=2)

    carry = h_ref[...]
    h_in = jnp.zeros((SUBLANES, W), F32)
    for s in (range(SUBLANES - 1, -1, -1) if reverse else range(SUBLANES)):
        h_in = jnp.where(sub == s, carry, h_in)
        carry = h_end[s:s + 1] + a_end[s:s + 1] * carry
    h_ref[...] = carry

    hs = u_ref[...].reshape(seg, SUBLANES, W) + a_ref[...].reshape(seg, SUBLANES, W) * h_in[None]
    hs = hs.reshape(ts, W)
    if final:
        o_ref[...] = ((hs + hb_ref[...]) * jax.nn.gelu(gate_ref[...])).astype(o_ref.dtype)
    else:
        o_ref[...] = hs


def _rec_scan(z, conv_w, conv_b, gate_w, gate_b, lam, j, direction, h_other, S, ts):
    T = z.shape[0]
    W = conv_w.shape[2]
    bw = W // LRU_BLOCKS
    nc = S // ts
    reverse = direction == 1
    final = h_other is not None
    assert ts % (2 * SUBLANES) == 0

    def chunk_of(c):
        return nc - 1 - c if reverse else c

    def main_map(b, c):
        return (b * nc + chunk_of(c), 1)

    def prev_map(b, c):
        return (jnp.maximum((b * nc + chunk_of(c)) * (ts // (2 * SUBLANES)) - 1, 0), 1)

    def next_map(b, c):
        return (jnp.minimum((b * nc + chunk_of(c) + 1) * (ts // SUBLANES), T // SUBLANES - 1), 1)

    in_specs = [
        pl.BlockSpec((2 * SUBLANES, W), prev_map),
        pl.BlockSpec((ts, W), main_map),
        pl.BlockSpec((SUBLANES, W), next_map),
        pl.BlockSpec((None, CONV_WIDTH, W), lambda b, c: (j, 0, 0)),
        pl.BlockSpec((None, 1, W), lambda b, c: (j, 0, 0)),
        pl.BlockSpec((None, None, 2, LRU_BLOCKS, bw, bw), lambda b, c: (j, direction, 0, 0, 0, 0)),
        pl.BlockSpec((None, None, 2, W), lambda b, c: (j, direction, 0, 0)),
        pl.BlockSpec((None, None, 1, W), lambda b, c: (j, direction, 0, 0)),
    ]
    args = [z, z, z, conv_w, conv_b, gate_w, gate_b, lam]
    if final:
        in_specs += [
            pl.BlockSpec((ts, W), lambda b, c: (b * nc + chunk_of(c), 0)),
            pl.BlockSpec((ts, W), lambda b, c: (b * nc + chunk_of(c), 0)),
        ]
        args += [z, h_other]
    kern = functools.partial(_rec_kernel, reverse=reverse, final=final, ts=ts, bw=bw)
    return pl.pallas_call(
        kern,
        grid=(T // S, nc),
        in_specs=in_specs,
        out_specs=pl.BlockSpec((ts, W), lambda b, c: (b * nc + chunk_of(c), 0)),
        out_shape=jax.ShapeDtypeStruct((T, W), BF16 if final else F32),
        scratch_shapes=[
            pltpu.VMEM((ts, W), F32),
            pltpu.VMEM((ts, W), F32),
            pltpu.VMEM((1, W), F32),
        ],
        compiler_params=_params("parallel", "arbitrary"),
        name="rec_fwd" if final else "rec_bwd",
    )(*args)


ATT_TQ = 128
ATT_INTERLEAVE = 16


def _alibi_slopes():
    n = ATT_GROUPS * ATT_HEADS
    s = [2.0 ** (-8.0 * (i + 1.0) / n) for i in range(n)]
    return np.asarray(s, np.float32).reshape(ATT_GROUPS, ATT_HEADS)


def _rows(start, size, stride):
    if stride > 1:
        return pl.ds(start, size, stride=stride)
    return pl.ds(start if isinstance(start, int) else pl.multiple_of(start, SUBLANES), size)


def _attn_kernel(slopes_ref, *refs, span, halves):
    o_ref, acc_o, acc_l = refs[7 * ATT_GROUPS:]
    head = pl.program_id(1)
    blk = pl.program_id(2)
    nblk = pl.num_programs(2)
    tq, hd = ATT_TQ, ATT_HEAD_DIM
    half = halves[0]
    nk = tq + 2 * half
    row = lax.broadcasted_iota(jnp.int32, (tq, nk), 0)
    col = lax.broadcasted_iota(jnp.int32, (tq, nk), 1)
    rel = jnp.abs(col - half - row)
    in_band = rel <= half
    rel_f = rel.astype(F32)
    col1 = lax.broadcasted_iota(jnp.int32, (1, nk), 1)
    lo_mask = jnp.where((col1 < half) & (blk == 0), NEG_INF, 0.0)
    hi_mask = jnp.where((col1 >= nk - half) & (blk == nblk - 1), NEG_INF, 0.0)
    scale = hd ** -0.5

    for g, (_, d) in enumerate(ATT_PATTERNS):
        assert halves[g] == half
        q_ref, kp_ref, kc_ref, kn_ref, vp_ref, vc_ref, vn_ref = refs[7 * g:7 * g + 7]
        nu = span // (tq * d)
        slope = slopes_ref[g, head]
        bias_mid = jnp.where(in_band, -slope * (d * rel_f), NEG_INF)
        biases = {"mid": bias_mid, "first": bias_mid + lo_mask, "last": bias_mid + hi_mask,
                  "only": bias_mid + lo_mask + hi_mask}

        def tiles(specs, g=g, d=d, q_ref=q_ref, kp_ref=kp_ref, kc_ref=kc_ref, kn_ref=kn_ref,
                  vp_ref=vp_ref, vc_ref=vc_ref, vn_ref=vn_ref, biases=biases):
            done = []
            for r, u, kind in specs:
                qrows = _rows(u * (tq * d) + r, tq, d)
                lo = (u * tq - half) * d + r
                if kind == "only":
                    pieces = [(kp_ref, vp_ref, _rows(r, half, d)), (kc_ref, vc_ref, _rows(r, tq, d)),
                              (kn_ref, vn_ref, _rows(r, half, d))]
                elif kind == "first":
                    pieces = [(kp_ref, vp_ref, _rows(r, half, d)), (kc_ref, vc_ref, _rows(r, tq + half, d))]
                elif kind == "last":
                    pieces = [(kc_ref, vc_ref, _rows(lo, tq + half, d)), (kn_ref, vn_ref, _rows(r, half, d))]
                else:
                    pieces = [(kc_ref, vc_ref, _rows(lo, nk, d))]
                k = jnp.concatenate([kr[rows, :] for kr, _, rows in pieces], axis=0).astype(BF16)
                v = jnp.concatenate([vr[rows, :] for _, vr, rows in pieces], axis=0).astype(BF16)
                q = q_ref[qrows, :].astype(BF16)
                s = _dot_nt(q, k) * scale + biases[kind]
                m = jnp.max(s, axis=-1, keepdims=True)
                e = jnp.exp(s - m)
                den = jnp.sum(e, axis=-1, keepdims=True)
                o = _dot((e / den).astype(BF16), v)
                lse = jnp.broadcast_to(m + jnp.log(den), (tq, hd))
                prev = (acc_o[qrows, :], acc_l[qrows, :]) if g > 0 else None
                done.append((qrows, o, lse, prev))
            for qrows, o, lse, prev in done:
                if prev is not None:
                    o_acc, lse_acc = prev
                    top = jnp.maximum(lse_acc, lse)
                    w_acc = jnp.exp(lse_acc - top)
                    w_cur = jnp.exp(lse - top)
                    tot = w_acc + w_cur
                    o = (w_acc * o_acc + w_cur * o) / tot
                    lse = top + jnp.log(tot)
                acc_o[qrows, :] = o
                if g < ATT_GROUPS - 1:
                    acc_l[qrows, :] = lse

        def kind_of(u, nu=nu):
            return "only" if nu == 1 else "first" if u == 0 else "last" if u == nu - 1 else "mid"

        ilv = ATT_INTERLEAVE
        if nu >= ilv:
            assert nu % ilv == 0
            nchunk = nu // ilv

            def per_class(r, nu=nu, nchunk=nchunk, tiles=tiles, kind_of=kind_of):
                tiles([(r, u, kind_of(u)) for u in range(ilv)])
                if nchunk > 2:
                    def mid(c, carry):
                        tiles([(r, c * ilv + t, "mid") for t in range(ilv)])
                        return carry
                    lax.fori_loop(1, nchunk - 1, mid, 0)
                if nchunk > 1:
                    tiles([(r, u, kind_of(u)) for u in range(nu - ilv, nu)])

            if d == 1:
                per_class(0)
            else:
                def classes(r, carry, per_class=per_class):
                    per_class(r)
                    return carry
                lax.fori_loop(0, d, classes, 0)
        else:
            per_blk = ilv // nu
            assert per_blk * nu == ilv and d % per_blk == 0

            def classes(rb, carry, nu=nu, per_blk=per_blk, tiles=tiles, kind_of=kind_of):
                tiles([(rb * per_blk + t, u, kind_of(u)) for t in range(per_blk) for u in range(nu)])
                return carry
            lax.fori_loop(0, d // per_blk, classes, 0)

    o_ref[...] = acc_o[...].astype(o_ref.dtype)


def _attention(zh, B, S, span):
    T = zh.shape[1]
    nb = S // span
    halves = tuple(w // (2 * d) for w, d in ATT_PATTERNS)
    in_specs = [pl.BlockSpec(memory_space=pltpu.SMEM)]
    for g, (_, d) in enumerate(ATT_PATTERNS):
        halo = halves[g] * d
        assert span % (ATT_TQ * d) == 0 and span % halo == 0 and S % span == 0
        ratio = span // halo

        def slab(p, g=g):
            return (3 * g + p) * ATT_HEADS

        def cur(p, slab=slab):
            return pl.BlockSpec((None, span, LANES), lambda b, h, i: (slab(p) + h, b * nb + i, 0))

        def before(p, slab=slab, halo=halo, ratio=ratio):
            return pl.BlockSpec((None, halo, LANES),
                                lambda b, h, i: (slab(p) + h, jnp.maximum((b * nb + i) * ratio - 1, 0), 0))

        def after(p, slab=slab, halo=halo, ratio=ratio):
            return pl.BlockSpec((None, halo, LANES),
                                lambda b, h, i: (slab(p) + h, jnp.minimum((b * nb + i + 1) * ratio, T // halo - 1), 0))

        in_specs += [cur(0), before(1), cur(1), after(1), before(2), cur(2), after(2)]
    kern = functools.partial(_attn_kernel, span=span, halves=halves)
    return pl.pallas_call(
        kern,
        grid=(B, ATT_HEADS, nb),
        in_specs=in_specs,
        out_specs=pl.BlockSpec((None, span, LANES), lambda b, h, i: (h, b * nb + i, 0)),
        out_shape=jax.ShapeDtypeStruct((ATT_HEADS, T, LANES), BF16),
        scratch_shapes=[pltpu.VMEM((span, LANES), F32), pltpu.VMEM((span, LANES), F32)],
        compiler_params=_params("parallel", "parallel", "arbitrary"),
        name="attn",
    )(jnp.asarray(_alibi_slopes()), *([zh] * (7 * ATT_GROUPS)))


def kernel(x, mem, ffn_norm, ffn_w_in, ffn_w_out, mix_norm, mem_norm, mem_w_kv, mem_qk_gain,
           rec_w_in, rec_conv_w, rec_conv_b, rec_gate_w, rec_gate_b, rec_lambda, rec_w_out,
           att_w_in, att_qk_gain, att_w_out):
    B, S, D = x.shape
    T = B * S
    depth = ffn_norm.shape[0]
    W = rec_conv_w.shape[2]

    tm_ffn = min(1024, S)
    tf = min(512, ffn_w_out.shape[2])
    tm_proj = min(1024, S)
    tn_proj = min(1024, D)
    ts_rec = min(256, S)
    span = ATT_TQ * max(d for _, d in ATT_PATTERNS)

    ffn_w_in, ffn_w_out, mem_w_kv, rec_w_in, rec_gate_w, rec_w_out, att_w_in, att_w_out = (
        w.astype(BF16) for w in (ffn_w_in, ffn_w_out, mem_w_kv, rec_w_in, rec_gate_w, rec_w_out,
                                 att_w_in, att_w_out))
    ffn_norm = ffn_norm[:, :, None, :]
    mix_norm = mix_norm[:, None, :]
    mem_norm = mem_norm[:, None, :]
    mem_qk_gain = mem_qk_gain[:, :, None, :]
    rec_conv_b = rec_conv_b[:, None, :]
    rec_gate_b = rec_gate_b.reshape(rec_gate_b.shape[0], 2, 2, W)
    rec_lambda = rec_lambda[:, :, None, :]
    att_qk_gain = att_qk_gain[:, :, :, None, :]

    x = x.reshape(T, D)
    for layer in range(depth):
        x = _ffn(x, ffn_norm, ffn_w_in, ffn_w_out, layer, 0, tm_ffn, tf)
        mk, mv = _memkv(mem, mem_norm, mem_w_kv, mem_qk_gain, layer)
        j = layer // 2
        if layer % 2 == 0:
            z = _inproj(x, mix_norm, rec_w_in, (layer, j), tm_proj, tn_proj, ts_rec)
            hb = _rec_scan(z, rec_conv_w, rec_conv_b, rec_gate_w, rec_gate_b, rec_lambda, j, 1, None, S, ts_rec)
            ya = _rec_scan(z, rec_conv_w, rec_conv_b, rec_gate_w, rec_gate_b, rec_lambda, j, 0, hb, S, ts_rec)
            ym = _memattn(z, 2 * W, mk, mv, mem_qk_gain, layer, S, tm_proj)
            x = _outproj(x, ya, ym, rec_w_out, j, tm_proj, tn_proj, ts_rec)
        else:
            zh = _inproj_heads(x, mix_norm, att_w_in, att_qk_gain, (layer, j), tm_proj, tn_proj)
            ya = _attention(zh, B, S, span)
            ym = _memattn(zh, ATT_GROUPS * 3 * ATT_WIDTH, mk, mv, mem_qk_gain, layer, S, tm_proj)
            x = _outproj(x, ya, ym, att_w_out, j, tm_proj, tn_proj)
        x = _ffn(x, ffn_norm, ffn_w_in, ffn_w_out, layer, 1, tm_ffn, tf)
    return x.reshape(B, S, D)
```

```python
import functools

import jax
import jax.numpy as jnp
import numpy as np
from jax import lax
from jax.experimental import pallas as pl
from jax.experimental.pallas import tpu as pltpu

F32 = jnp.float32
BF16 = jnp.bfloat16

EPS = 1e-6
NEG_INF = -1e30
LRU_C = 8.0
LRU_BLOCKS = 8
CONV_WIDTH = 4
CONV_LEFT = CONV_WIDTH // 2
ATT_PATTERNS = ((128, 1), (512, 4), (2048, 16))
ATT_GROUPS = len(ATT_PATTERNS)
ATT_HEADS = 8
ATT_HEAD_DIM = 128
ATT_WIDTH = ATT_HEADS * ATT_HEAD_DIM
MEM_HEADS = 4
SUBLANES = 8
LANES = 128
BF16_ROWS = 16
VMEM_LIMIT_BYTES = 60 * 1024 * 1024


def _params(*semantics):
    return pltpu.CompilerParams(dimension_semantics=semantics, vmem_limit_bytes=VMEM_LIMIT_BYTES)


def _rms(x, g):
    return x * lax.rsqrt(jnp.mean(x * x, axis=-1, keepdims=True) + EPS) * g


def _dot(a, b):
    return jnp.dot(a, b, preferred_element_type=F32)


def _dot_nt(a, b):
    return lax.dot_general(a, b, (((1,), (1,)), ((), ())), preferred_element_type=F32)


def _ffn_kernel(*refs, n_casts):
    x_ref, g_ref, wg_ref, wu_ref, wo_ref = refs[:5]
    src_refs = refs[5:5 + n_casts]
    o_ref = refs[5 + n_casts]
    dst_refs = refs[6 + n_casts:6 + 2 * n_casts]
    xn_ref = refs[6 + 2 * n_casts]
    j = pl.program_id(1)

    @pl.when(j == 0)
    def _():
        xn_ref[...] = _rms(x_ref[...], g_ref[...]).astype(BF16)
        o_ref[...] = jnp.zeros_like(o_ref)

    xn = xn_ref[...]
    gate = _dot(xn, wg_ref[...])
    up = _dot(xn, wu_ref[...])
    h = (jax.nn.silu(gate) * up).astype(BF16)
    o_ref[...] += _dot(h, wo_ref[...])

    for src_ref, dst_ref in zip(src_refs, dst_refs):
        dst_ref[...] = src_ref[...].astype(BF16)

    @pl.when(j == pl.num_programs(1) - 1)
    def _():
        o_ref[...] = x_ref[...] + 0.5 * o_ref[...]


def _cast_rows(rows, steps):
    for rb in range(BF16_ROWS, rows + 1, BF16_ROWS):
        if rows % rb == 0 and rows // rb <= steps:
            return rb
    raise ValueError((rows, steps))


def _ffn(x, g, w_in, w_out, layer, k, tm, tf, casts=()):
    T, D = x.shape
    F = w_out.shape[2]
    nf = F // tf
    steps = (T // tm) * nf
    in_specs = [
        pl.BlockSpec((tm, D), lambda i, j: (i, 0)),
        pl.BlockSpec((None, None, 1, D), lambda i, j: (layer, k, 0, 0)),
        pl.BlockSpec((None, None, D, tf), lambda i, j: (0, 0, 0, j)),
        pl.BlockSpec((None, None, D, tf), lambda i, j: (0, 0, 0, j + nf)),
        pl.BlockSpec((None, None, tf, D), lambda i, j: (0, 0, j, 0)),
    ]
    out_specs = [pl.BlockSpec((tm, D), lambda i, j: (i, 0))]
    out_shape = [jax.ShapeDtypeStruct((T, D), F32)]
    for arr, lead in casts:
        rows, cols = arr.shape[-2:]
        rb = _cast_rows(rows, steps)
        nblk = rows // rb
        squeezed = (None,) * len(lead)

        def src_map(i, j, lead=lead, nblk=nblk):
            return (*lead, jnp.minimum(i * nf + j, nblk - 1), 0)

        def dst_map(i, j, lead=lead, nblk=nblk):
            return (*((0,) * len(lead)), jnp.minimum(i * nf + j, nblk - 1), 0)

        in_specs.append(pl.BlockSpec((*squeezed, rb, cols), src_map))
        out_specs.append(pl.BlockSpec((*squeezed, rb, cols), dst_map))
        out_shape.append(jax.ShapeDtypeStruct((1,) * len(lead) + (rows, cols), BF16))
    outs = pl.pallas_call(
        functools.partial(_ffn_kernel, n_casts=len(casts)),
        grid=(T // tm, nf),
        in_specs=in_specs,
        out_specs=out_specs,
        out_shape=out_shape,
        scratch_shapes=[pltpu.VMEM((tm, D), BF16)],
        compiler_params=_params("parallel", "arbitrary"),
        name="ffn",
    )(x, g, w_in, w_in, w_out, *(arr for arr, _ in casts))
    return outs[0], outs[1:]


def _segment_major_perm(chunk, inverse):
    seg = chunk // SUBLANES
    i = lax.broadcasted_iota(jnp.int32, (chunk, chunk), 0)
    j = lax.broadcasted_iota(jnp.int32, (chunk, chunk), 1)
    src = (i % seg) * SUBLANES + i // seg if inverse else (i % SUBLANES) * seg + i // SUBLANES
    return jnp.where(j == src, 1.0, 0.0).astype(BF16)


def _permute_chunks(dst_ref, src, chunk, inverse):
    perm = _segment_major_perm(chunk, inverse)
    for c in range(src.shape[0] // chunk):
        rows = slice(c * chunk, (c + 1) * chunk)
        dst_ref[rows, :] = _dot(perm, src[rows, :]).astype(BF16)


def _inproj_kernel(x_ref, g_ref, w_ref, o_ref, xn_ref, *, chunk):
    @pl.when(pl.program_id(1) == 0)
    def _():
        xn = _rms(x_ref[...], g_ref[...]).astype(BF16)
        _permute_chunks(xn_ref, xn, chunk, inverse=False)

    o_ref[...] = _dot(xn_ref[...], w_ref[...])


def _inproj(x, g, w, layer, tm, tn, chunk):
    T, D = x.shape
    N = w.shape[2]
    assert tm % chunk == 0
    return pl.pallas_call(
        functools.partial(_inproj_kernel, chunk=chunk),
        grid=(T // tm, N // tn),
        in_specs=[
            pl.BlockSpec((tm, D), lambda i, j: (i, 0)),
            pl.BlockSpec((None, 1, D), lambda i, j: (layer, 0, 0)),
            pl.BlockSpec((None, D, tn), lambda i, j: (0, 0, j)),
        ],
        out_specs=pl.BlockSpec((tm, tn), lambda i, j: (i, j)),
        out_shape=jax.ShapeDtypeStruct((T, N), F32),
        scratch_shapes=[pltpu.VMEM((tm, D), BF16)],
        compiler_params=_params("parallel", "arbitrary"),
        name="inproj",
    )(x, g, w)


def _inproj_heads_kernel(x_ref, g_ref, w_ref, qkg_ref, o_ref, xn_ref, *, tiles_per_part):
    j = pl.program_id(1)

    @pl.when(j == 0)
    def _():
        xn_ref[...] = _rms(x_ref[...], g_ref[...]).astype(BF16)

    z = _dot(xn_ref[...], w_ref[...])
    part = j // tiles_per_part
    is_qk = (part < 3 * ATT_GROUPS) & (part % 3 < 2)
    gain = jnp.where(is_qk, qkg_ref[...], 1.0)
    for h in range(o_ref.shape[0]):
        zh = z[:, h * LANES:(h + 1) * LANES]
        inv = lax.rsqrt(jnp.mean(zh * zh, axis=-1, keepdims=True) + EPS)
        o_ref[h] = zh * jnp.where(is_qk, inv, 1.0) * gain


def _inproj_heads(x, g, w, qk_gain, layer, att_layer, tm, tn):
    T, D = x.shape
    N = w.shape[2]
    assert ATT_HEAD_DIM == LANES and tn % LANES == 0 and ATT_WIDTH % tn == 0
    tpp = ATT_WIDTH // tn

    def gain_map(i, j):
        part = j // tpp
        return (att_layer, jnp.minimum(part % 3, 1), jnp.minimum(part // 3, ATT_GROUPS - 1), 0, 0)

    kern = functools.partial(_inproj_heads_kernel, tiles_per_part=tpp)
    return pl.pallas_call(
        kern,
        grid=(T // tm, N // tn),
        in_specs=[
            pl.BlockSpec((tm, D), lambda i, j: (i, 0)),
            pl.BlockSpec((None, 1, D), lambda i, j: (layer, 0, 0)),
            pl.BlockSpec((None, D, tn), lambda i, j: (0, 0, j)),
            pl.BlockSpec((None, None, None, 1, LANES), gain_map),
        ],
        out_specs=pl.BlockSpec((tn // LANES, tm, LANES), lambda i, j: (j, i, 0)),
        out_shape=jax.ShapeDtypeStruct((N // LANES, T, LANES), F32),
        scratch_shapes=[pltpu.VMEM((tm, D), BF16)],
        compiler_params=_params("parallel", "arbitrary"),
        name="inproj_heads",
    )(x, g, w, qk_gain)


def _memkv_kernel(mem_ref, g_ref, w_ref, kg_ref, k_ref, v_ref, *, heads, hd):
    mem_n = _rms(mem_ref[...], g_ref[...]).astype(BF16)
    kv = _dot(mem_n, w_ref[...])
    for h in range(heads):
        sl = slice(h * hd, (h + 1) * hd)
        k_ref[:, sl] = _rms(kv[:, sl], kg_ref[...]).astype(BF16)
    v_ref[...] = kv[:, heads * hd:].astype(BF16)


def _memkv(mem, g, w_kv, qk_gain, layer):
    B, M, D = mem.shape
    mw = w_kv.shape[2] // 2
    hd = mw // MEM_HEADS
    kern = functools.partial(_memkv_kernel, heads=MEM_HEADS, hd=hd)
    return pl.pallas_call(
        kern,
        grid=(B,),
        in_specs=[
            pl.BlockSpec((None, M, D), lambda b: (b, 0, 0)),
            pl.BlockSpec((None, 1, D), lambda b: (layer, 0, 0)),
            pl.BlockSpec((None, D, 2 * mw), lambda b: (0, 0, 0)),
            pl.BlockSpec((None, None, 1, hd), lambda b: (layer, 1, 0, 0)),
        ],
        out_specs=[pl.BlockSpec((None, M, mw), lambda b: (b, 0, 0))] * 2,
        out_shape=[jax.ShapeDtypeStruct((B, M, mw), BF16)] * 2,
        compiler_params=_params("parallel"),
        name="memkv",
    )(mem, g, w_kv, qk_gain)


def _memattn_kernel(q_ref, qg_ref, k_ref, v_ref, o_ref, *, heads, hd, head_major):
    scale = hd ** -0.5
    per = hd // LANES
    for h in range(heads):
        sl = slice(h * hd, (h + 1) * hd)
        if head_major:
            qh = jnp.concatenate([q_ref[h * per + c] for c in range(per)], axis=1)
        else:
            qh = q_ref[:, sl]
        q = _rms(qh, qg_ref[...]).astype(BF16)
        s = _dot_nt(q, k_ref[:, sl]) * scale
        e = jnp.exp(s - jnp.max(s, axis=-1, keepdims=True))
        p = e / jnp.sum(e, axis=-1, keepdims=True)
        o_ref[:, sl] = _dot(p.astype(BF16), v_ref[:, sl]).astype(o_ref.dtype)


def _memattn(z, q_col, mk, mv, qk_gain, layer, S, tm):
    head_major = z.ndim == 3
    T = z.shape[1] if head_major else z.shape[0]
    _, M, mw = mk.shape
    hd = mw // MEM_HEADS
    qb = q_col // mw
    assert qb * mw == q_col and hd % LANES == 0
    per_b = S // tm
    if head_major:
        q_spec = pl.BlockSpec((mw // LANES, tm, LANES), lambda i: (qb, i, 0))
    else:
        q_spec = pl.BlockSpec((tm, mw), lambda i: (i, qb))
    kern = functools.partial(_memattn_kernel, heads=MEM_HEADS, hd=hd, head_major=head_major)
    return pl.pallas_call(
        kern,
        grid=(T // tm,),
        in_specs=[
            q_spec,
            pl.BlockSpec((None, None, 1, hd), lambda i: (layer, 0, 0, 0)),
            pl.BlockSpec((None, M, mw), lambda i: (i // per_b, 0, 0)),
            pl.BlockSpec((None, M, mw), lambda i: (i // per_b, 0, 0)),
        ],
        out_specs=pl.BlockSpec((tm, mw), lambda i: (i, 0)),
        out_shape=jax.ShapeDtypeStruct((T, mw), BF16),
        compiler_params=_params("parallel"),
        name="memattn",
    )(z, qk_gain, mk, mv)


def _outproj_kernel(ya_ref, ym_ref, wa_ref, wm_ref, x_ref, o_ref, ya_seq, ym_seq, *, chunk):
    @pl.when(pl.program_id(1) == 0)
    def _():
        _permute_chunks(ya_seq, ya_ref[...], chunk, inverse=True)
        _permute_chunks(ym_seq, ym_ref[...], chunk, inverse=True)

    o_ref[...] = x_ref[...] + _dot(ya_seq[...], wa_ref[...]) + _dot(ym_seq[...], wm_ref[...])


def _outproj_heads_kernel(ya_ref, ym_ref, wa_ref, wm_ref, x_ref, o_ref, cat_ref):
    @pl.when(pl.program_id(1) == 0)
    def _():
        for h in range(ya_ref.shape[0]):
            cat_ref[:, h * LANES:(h + 1) * LANES] = ya_ref[h]

    o_ref[...] = x_ref[...] + _dot(cat_ref[...], wa_ref[...]) + _dot(ym_ref[...], wm_ref[...])


def _outproj(x, ya, ym, w, tm, tn, chunk=None):
    T, D = x.shape
    head_major = ya.ndim == 3
    assert head_major == (chunk is None)
    ka = ya.shape[0] * LANES if head_major else ya.shape[1]
    km = ym.shape[1]
    mb = ka // km
    assert mb * km == ka
    if head_major:
        ya_spec = pl.BlockSpec((ka // LANES, tm, LANES), lambda i, j: (0, i, 0))
        kern, scratch = _outproj_heads_kernel, [pltpu.VMEM((tm, ka), BF16)]
    else:
        assert tm % chunk == 0
        ya_spec = pl.BlockSpec((tm, ka), lambda i, j: (i, 0))
        kern = functools.partial(_outproj_kernel, chunk=chunk)
        scratch = [pltpu.VMEM((tm, ka), BF16), pltpu.VMEM((tm, km), BF16)]
    return pl.pallas_call(
        kern,
        grid=(T // tm, D // tn),
        in_specs=[
            ya_spec,
            pl.BlockSpec((tm, km), lambda i, j: (i, 0)),
            pl.BlockSpec((None, ka, tn), lambda i, j: (0, 0, j)),
            pl.BlockSpec((None, km, tn), lambda i, j: (0, mb, j)),
            pl.BlockSpec((tm, tn), lambda i, j: (i, j)),
        ],
        out_specs=pl.BlockSpec((tm, tn), lambda i, j: (i, j)),
        out_shape=jax.ShapeDtypeStruct((T, D), F32),
        scratch_shapes=scratch,
        compiler_params=_params("parallel", "arbitrary"),
        name="outproj",
    )(ya, ym, w, w, x)


def _rec_kernel(*refs, reverse, final, ts, bw):
    if final:
        (prev_ref, main_ref, next_ref, cw_ref, cb_ref, gw_ref, gb_ref, lam_ref,
         gate_ref, hb_ref, o_ref, a_ref, u_ref, h_ref) = refs
    else:
        (prev_ref, main_ref, next_ref, cw_ref, cb_ref, gw_ref, gb_ref, lam_ref,
         o_ref, a_ref, u_ref, h_ref) = refs
    c = pl.program_id(1)
    nc = pl.num_programs(1)
    chunk = nc - 1 - c if reverse else c
    seg = ts // SUBLANES
    W = main_ref.shape[1]
    assert CONV_WIDTH == 4 and CONV_LEFT == 2

    @pl.when(c == 0)
    def _():
        h_ref[...] = jnp.zeros_like(h_ref)

    main = main_ref[...]
    prev = jnp.where(chunk > 0, prev_ref[...], 0.0)
    nxt = jnp.where(chunk < nc - 1, next_ref[...], 0.0)
    sub = lax.broadcasted_iota(jnp.int32, (SUBLANES, W), 0)

    def from_prev_segment(tile, fill):
        return jnp.where(sub == 0, fill, pltpu.roll(tile, 1, axis=0))

    def from_next_segment(tile, fill):
        return jnp.where(sub == SUBLANES - 1, fill, pltpu.roll(tile, SUBLANES - 1, axis=0))

    back1 = from_prev_segment(main[ts - SUBLANES:], prev[2 * SUBLANES - 1:])
    back2 = from_prev_segment(main[ts - 2 * SUBLANES:ts - SUBLANES], prev[SUBLANES - 1:SUBLANES])
    ahead1 = from_next_segment(main[:SUBLANES], nxt[0:1])
    taps = (jnp.concatenate([back2, back1, main[:ts - 2 * SUBLANES]], axis=0),
            jnp.concatenate([back1, main[:ts - SUBLANES]], axis=0),
            main,
            jnp.concatenate([main[SUBLANES:], ahead1], axis=0))
    xc = cb_ref[...]
    for k in range(CONV_WIDTH):
        xc = xc + cw_ref[k:k + 1, :] * taps[k]

    log_a_scale = (-0.5 * LRU_C) * jax.nn.softplus(-lam_ref[...])
    xh = 0.5 * xc
    xhb = xh.astype(BF16)
    gbh = 0.5 * gb_ref[...]
    for n in range(LRU_BLOCKS):
        sl = slice(n * bw, (n + 1) * bw)
        tr = jnp.tanh(_dot(xhb[:, sl], gw_ref[0, n]) + gbh[0:1, sl])
        ti = jnp.tanh(_dot(xhb[:, sl], gw_ref[1, n]) + gbh[1:2, sl])
        a = jnp.exp((tr + 1.0) * log_a_scale[:, sl])
        y = 1.0 - a * a
        a_ref[:, sl] = a
        u_ref[:, sl] = (y * lax.rsqrt(jnp.maximum(y, 1e-30))) * ((ti + 1.0) * xh[:, sl])

    def step(t, carry):
        h, prod = carry
        k = seg - 1 - t if reverse else t
        rows = pl.ds(pl.multiple_of(k * SUBLANES, SUBLANES), SUBLANES)
        a = a_ref[rows, :]
        h = a * h + u_ref[rows, :]
        prod = a * prod
        u_ref[rows, :] = h
        a_ref[rows, :] = prod
        return h, prod

    h_end, a_end = lax.fori_loop(0, seg, step, (jnp.zeros((SUBLANES, W), F32), jnp.ones((SUBLANES, W), F32)),
                                 unroll=2)

    carry = h_ref[...]
    h_in = jnp.zeros((SUBLANES, W), F32)
    for s in (range(SUBLANES - 1, -1, -1) if reverse else range(SUBLANES)):
        h_in = jnp.where(sub == s, carry, h_in)
        carry = h_end[s:s + 1] + a_end[s:s + 1] * carry
    h_ref[...] = carry

    hs = u_ref[...].reshape(seg, SUBLANES, W) + a_ref[...].reshape(seg, SUBLANES, W) * h_in[None]
    hs = hs.reshape(ts, W)
    if final:
        o_ref[...] = ((hs + hb_ref[...]) * jax.nn.gelu(gate_ref[...])).astype(o_ref.dtype)
    else:
        o_ref[...] = hs


def _rec_scan(z, conv_w, conv_b, gate_w, gate_b, lam, j, direction, h_other, S, ts):
    T = z.shape[0]
    W = conv_w.shape[2]
    bw = W // LRU_BLOCKS
    nc = S // ts
    reverse = direction == 1
    final = h_other is not None
    assert ts % (2 * SUBLANES) == 0

    def chunk_of(c):
        return nc - 1 - c if reverse else c

    def main_map(b, c):
        return (b * nc + chunk_of(c), 1)

    def prev_map(b, c):
        return (jnp.maximum((b * nc + chunk_of(c)) * (ts // (2 * SUBLANES)) - 1, 0), 1)

    def next_map(b, c):
        return (jnp.minimum((b * nc + chunk_of(c) + 1) * (ts // SUBLANES), T // SUBLANES - 1), 1)

    in_specs = [
        pl.BlockSpec((2 * SUBLANES, W), prev_map),
        pl.BlockSpec((ts, W), main_map),
        pl.BlockSpec((SUBLANES, W), next_map),
        pl.BlockSpec((None, CONV_WIDTH, W), lambda b, c: (j, 0, 0)),
        pl.BlockSpec((None, 1, W), lambda b, c: (j, 0, 0)),
        pl.BlockSpec((None, None, 2, LRU_BLOCKS, bw, bw), lambda b, c: (0, direction, 0, 0, 0, 0)),
        pl.BlockSpec((None, None, 2, W), lambda b, c: (j, direction, 0, 0)),
        pl.BlockSpec((None, None, 1, W), lambda b, c: (j, direction, 0, 0)),
    ]
    args = [z, z, z, conv_w, conv_b, gate_w, gate_b, lam]
    if final:
        in_specs += [
            pl.BlockSpec((ts, W), lambda b, c: (b * nc + chunk_of(c), 0)),
            pl.BlockSpec((ts, W), lambda b, c: (b * nc + chunk_of(c), 0)),
        ]
        args += [z, h_other]
    kern = functools.partial(_rec_kernel, reverse=reverse, final=final, ts=ts, bw=bw)
    return pl.pallas_call(
        kern,
        grid=(T // S, nc),
        in_specs=in_specs,
        out_specs=pl.BlockSpec((ts, W), lambda b, c: (b * nc + chunk_of(c), 0)),
        out_shape=jax.ShapeDtypeStruct((T, W), BF16 if final else F32),
        scratch_shapes=[
            pltpu.VMEM((ts, W), F32),
            pltpu.VMEM((ts, W), F32),
            pltpu.VMEM((1, W), F32),
        ],
        compiler_params=_params("parallel", "arbitrary"),
        name="rec_fwd" if final else "rec_bwd",
    )(*args)


ATT_TQ = 128
ATT_INTERLEAVE = 16


def _alibi_slopes():
    n = ATT_GROUPS * ATT_HEADS
    s = [2.0 ** (-8.0 * (i + 1.0) / n) for i in range(n)]
    return np.asarray(s, np.float32).reshape(ATT_GROUPS, ATT_HEADS)


def _rows(start, size, stride):
    if stride > 1:
        return pl.ds(start, size, stride=stride)
    return pl.ds(start if isinstance(start, int) else pl.multiple_of(start, SUBLANES), size)


def _attn_kernel(slopes_ref, *refs, span, halves):
    o_ref, acc_o, acc_l = refs[7 * ATT_GROUPS:]
    head = pl.program_id(1)
    blk = pl.program_id(2)
    nblk = pl.num_programs(2)
    tq, hd = ATT_TQ, ATT_HEAD_DIM
    half = halves[0]
    nk = tq + 2 * half
    row = lax.broadcasted_iota(jnp.int32, (tq, nk), 0)
    col = lax.broadcasted_iota(jnp.int32, (tq, nk), 1)
    rel = jnp.abs(col - half - row)
    in_band = rel <= half
    rel_f = rel.astype(F32)
    col1 = lax.broadcasted_iota(jnp.int32, (1, nk), 1)
    lo_mask = jnp.where((col1 < half) & (blk == 0), NEG_INF, 0.0)
    hi_mask = jnp.where((col1 >= nk - half) & (blk == nblk - 1), NEG_INF, 0.0)
    scale = hd ** -0.5

    for g, (_, d) in enumerate(ATT_PATTERNS):
        assert halves[g] == half
        q_ref, kp_ref, kc_ref, kn_ref, vp_ref, vc_ref, vn_ref = refs[7 * g:7 * g + 7]
        nu = span // (tq * d)
        slope = slopes_ref[g, head]
        bias_mid = jnp.where(in_band, -slope * (d * rel_f), NEG_INF)
        biases = {"mid": bias_mid, "first": bias_mid + lo_mask, "last": bias_mid + hi_mask,
                  "only": bias_mid + lo_mask + hi_mask}

        def tiles(specs, g=g, d=d, q_ref=q_ref, kp_ref=kp_ref, kc_ref=kc_ref, kn_ref=kn_ref,
                  vp_ref=vp_ref, vc_ref=vc_ref, vn_ref=vn_ref, biases=biases):
            done = []
            for r, u, kind in specs:
                qrows = _rows(u * (tq * d) + r, tq, d)
                lo = (u * tq - half) * d + r
                if kind == "only":
                    pieces = [(kp_ref, vp_ref, _rows(r, half, d)), (kc_ref, vc_ref, _rows(r, tq, d)),
                              (kn_ref, vn_ref, _rows(r, half, d))]
                elif kind == "first":
                    pieces = [(kp_ref, vp_ref, _rows(r, half, d)), (kc_ref, vc_ref, _rows(r, tq + half, d))]
                elif kind == "last":
                    pieces = [(kc_ref, vc_ref, _rows(lo, tq + half, d)), (kn_ref, vn_ref, _rows(r, half, d))]
                else:
                    pieces = [(kc_ref, vc_ref, _rows(lo, nk, d))]
                k = jnp.concatenate([kr[rows, :] for kr, _, rows in pieces], axis=0).astype(BF16)
                v = jnp.concatenate([vr[rows, :] for _, vr, rows in pieces], axis=0).astype(BF16)
                q = q_ref[qrows, :].astype(BF16)
                s = _dot_nt(q, k) * scale + biases[kind]
                m = jnp.max(s, axis=-1, keepdims=True)
                e = jnp.exp(s - m)
                den = jnp.sum(e, axis=-1, keepdims=True)
                o = _dot((e / den).astype(BF16), v)
                lse = jnp.broadcast_to(m + jnp.log(den), (tq, hd))
                prev = (acc_o[qrows, :], acc_l[qrows, :]) if g > 0 else None
                done.append((qrows, o, lse, prev))
            for qrows, o, lse, prev in done:
                if prev is not None:
                    o_acc, lse_acc = prev
                    top = jnp.maximum(lse_acc, lse)
                    w_acc = jnp.exp(lse_acc - top)
                    w_cur = jnp.exp(lse - top)
                    tot = w_acc + w_cur
                    o = (w_acc * o_acc + w_cur * o) / tot
                    lse = top + jnp.log(tot)
                acc_o[qrows, :] = o
                if g < ATT_GROUPS - 1:
                    acc_l[qrows, :] = lse

        def kind_of(u, nu=nu):
            return "only" if nu == 1 else "first" if u == 0 else "last" if u == nu - 1 else "mid"

        ilv = ATT_INTERLEAVE
        if nu >= ilv:
            assert nu % ilv == 0
            nchunk = nu // ilv

            def per_class(r, nu=nu, nchunk=nchunk, tiles=tiles, kind_of=kind_of):
                tiles([(r, u, kind_of(u)) for u in range(ilv)])
                if nchunk > 2:
                    def mid(c, carry):
                        tiles([(r, c * ilv + t, "mid") for t in range(ilv)])
                        return carry
                    lax.fori_loop(1, nchunk - 1, mid, 0)
                if nchunk > 1:
                    tiles([(r, u, kind_of(u)) for u in range(nu - ilv, nu)])

            if d == 1:
                per_class(0)
            else:
                def classes(r, carry, per_class=per_class):
                    per_class(r)
                    return carry
                lax.fori_loop(0, d, classes, 0)
        else:
            per_blk = ilv // nu
            assert per_blk * nu == ilv and d % per_blk == 0

            def classes(rb, carry, nu=nu, per_blk=per_blk, tiles=tiles, kind_of=kind_of):
                tiles([(rb * per_blk + t, u, kind_of(u)) for t in range(per_blk) for u in range(nu)])
                return carry
            lax.fori_loop(0, d // per_blk, classes, 0)

    o_ref[...] = acc_o[...].astype(o_ref.dtype)


def _attention(zh, B, S, span):
    T = zh.shape[1]
    nb = S // span
    halves = tuple(w // (2 * d) for w, d in ATT_PATTERNS)
    in_specs = [pl.BlockSpec(memory_space=pltpu.SMEM)]
    for g, (_, d) in enumerate(ATT_PATTERNS):
        halo = halves[g] * d
        assert span % (ATT_TQ * d) == 0 and span % halo == 0 and S % span == 0
        ratio = span // halo

        def slab(p, g=g):
            return (3 * g + p) * ATT_HEADS

        def cur(p, slab=slab):
            return pl.BlockSpec((None, span, LANES), lambda b, h, i: (slab(p) + h, b * nb + i, 0))

        def before(p, slab=slab, halo=halo, ratio=ratio):
            return pl.BlockSpec((None, halo, LANES),
                                lambda b, h, i: (slab(p) + h, jnp.maximum((b * nb + i) * ratio - 1, 0), 0))

        def after(p, slab=slab, halo=halo, ratio=ratio):
            return pl.BlockSpec((None, halo, LANES),
                                lambda b, h, i: (slab(p) + h, jnp.minimum((b * nb + i + 1) * ratio, T // halo - 1), 0))

        in_specs += [cur(0), before(1), cur(1), after(1), before(2), cur(2), after(2)]
    kern = functools.partial(_attn_kernel, span=span, halves=halves)
    return pl.pallas_call(
        kern,
        grid=(B, ATT_HEADS, nb),
        in_specs=in_specs,
        out_specs=pl.BlockSpec((None, span, LANES), lambda b, h, i: (h, b * nb + i, 0)),
        out_shape=jax.ShapeDtypeStruct((ATT_HEADS, T, LANES), BF16),
        scratch_shapes=[pltpu.VMEM((span, LANES), F32), pltpu.VMEM((span, LANES), F32)],
        compiler_params=_params("parallel", "parallel", "arbitrary"),
        name="attn",
    )(jnp.asarray(_alibi_slopes()), *([zh] * (7 * ATT_GROUPS)))


def kernel(x, mem, ffn_norm, ffn_w_in, ffn_w_out, mix_norm, mem_norm, mem_w_kv, mem_qk_gain,
           rec_w_in, rec_conv_w, rec_conv_b, rec_gate_w, rec_gate_b, rec_lambda, rec_w_out,
           att_w_in, att_qk_gain, att_w_out):
    B, S, D = x.shape
    T = B * S
    depth = ffn_norm.shape[0]
    W = rec_conv_w.shape[2]

    tm_ffn = min(1024, S)
    tf = min(512, ffn_w_out.shape[2])
    tm_proj = min(1024, S)
    tn_proj = min(1024, D)
    ts_rec = min(256, S)
    span = ATT_TQ * max(d for _, d in ATT_PATTERNS)

    ffn_norm = ffn_norm[:, :, None, :]
    mix_norm = mix_norm[:, None, :]
    mem_norm = mem_norm[:, None, :]
    mem_qk_gain = mem_qk_gain[:, :, None, :]
    rec_conv_b = rec_conv_b[:, None, :]
    rec_gate_b = rec_gate_b.reshape(rec_gate_b.shape[0], 2, 2, W)
    rec_lambda = rec_lambda[:, :, None, :]
    att_qk_gain = att_qk_gain[:, :, :, None, :]
    gate_w_rows = rec_gate_w.reshape(rec_gate_w.shape[0], -1, rec_gate_w.shape[-1])

    def mixer_casts(layer):
        j = layer // 2
        if layer % 2 == 0:
            return [(rec_w_in, (j,)), (mem_w_kv, (layer,))], [(rec_w_out, (j,)), (gate_w_rows, (j,))]
        return [(att_w_in, (j,)), (mem_w_kv, (layer,))], [(att_w_out, (j,))]

    def cast_now(arr, lead):
        return arr[tuple(slice(i, i + 1) for i in lead)].astype(BF16)

    ffn_w = [cast_now(ffn_w_in, (0, 0)), cast_now(ffn_w_out, (0, 0))]
    mix_w = [cast_now(*c) for part in mixer_casts(0) for c in part]

    x = x.reshape(T, D)
    for layer in range(depth):
        more = layer + 1 < depth
        early, late = mixer_casts(layer + 1) if more else ([], [])
        x, done = _ffn(x, ffn_norm, *ffn_w, layer, 0, tm_ffn, tf,
                       [(ffn_w_in, (layer, 1)), (ffn_w_out, (layer, 1))] + early)
        ffn_w, next_mix_w = done[:2], list(done[2:])
        w_in, w_kv, w_out, *w_gate = mix_w
        mk, mv = _memkv(mem, mem_norm, w_kv, mem_qk_gain, layer)
        j = layer // 2
        if layer % 2 == 0:
            gate_w = w_gate[0].reshape((1,) + rec_gate_w.shape[1:])
            z = _inproj(x, mix_norm, w_in, layer, tm_proj, tn_proj, ts_rec)
            hb = _rec_scan(z, rec_conv_w, rec_conv_b, gate_w, rec_gate_b, rec_lambda, j, 1, None, S, ts_rec)
            ya = _rec_scan(z, rec_conv_w, rec_conv_b, gate_w, rec_gate_b, rec_lambda, j, 0, hb, S, ts_rec)
            ym = _memattn(z, 2 * W, mk, mv, mem_qk_gain, layer, S, tm_proj)
            x = _outproj(x, ya, ym, w_out, tm_proj, tn_proj, ts_rec)
        else:
            zh = _inproj_heads(x, mix_norm, w_in, att_qk_gain, layer, j, tm_proj, tn_proj)
            ya = _attention(zh, B, S, span)
            ym = _memattn(zh, ATT_GROUPS * 3 * ATT_WIDTH, mk, mv, mem_qk_gain, layer, S, tm_proj)
            x = _outproj(x, ya, ym, w_out, tm_proj, tn_proj)
        x, done = _ffn(x, ffn_norm, *ffn_w, layer, 1, tm_ffn, tf,
                       [(ffn_w_in, (layer + 1, 0)), (ffn_w_out, (layer + 1, 0))] + late if more else [])
        ffn_w, mix_w = done[:2], next_mix_w + list(done[2:])
    return x.reshape(B, S, D)
```

```python
import functools

import jax
import jax.numpy as jnp
import numpy as np
from jax import lax
from jax.experimental import pallas as pl
from jax.experimental.pallas import tpu as pltpu

F32 = jnp.float32
BF16 = jnp.bfloat16

EPS = 1e-6
NEG_INF = -1e30
LRU_C = 8.0
LRU_BLOCKS = 8
CONV_WIDTH = 4
CONV_LEFT = CONV_WIDTH // 2
ATT_PATTERNS = ((128, 1), (512, 4), (2048, 16))
ATT_GROUPS = len(ATT_PATTERNS)
ATT_HEADS = 8
ATT_HEAD_DIM = 128
ATT_WIDTH = ATT_HEADS * ATT_HEAD_DIM
MEM_HEADS = 4
SUBLANES = 8
LANES = 128
BF16_ROWS = 16
VMEM_LIMIT_BYTES = 62 * 1024 * 1024


def _params(*semantics):
    return pltpu.CompilerParams(dimension_semantics=semantics, vmem_limit_bytes=VMEM_LIMIT_BYTES)


def _rms(x, g):
    return x * lax.rsqrt(jnp.mean(x * x, axis=-1, keepdims=True) + EPS) * g


def _dot(a, b):
    return jnp.dot(a, b, preferred_element_type=F32)


def _dot_nt(a, b):
    return lax.dot_general(a, b, (((1,), (1,)), ((), ())), preferred_element_type=F32)


def _ffn_kernel(*refs, n_casts):
    x_ref, g_ref, wg_ref, wu_ref, wo_ref = refs[:5]
    src_refs = refs[5:5 + n_casts]
    o_ref = refs[5 + n_casts]
    dst_refs = refs[6 + n_casts:6 + 2 * n_casts]
    xn_ref, h_ref = refs[6 + 2 * n_casts:]
    j = pl.program_id(1)
    last = pl.num_programs(1) - 1

    def hidden_chunk():
        xn = xn_ref[...]
        gate = _dot(xn, wg_ref[...])
        up = _dot(xn, wu_ref[...])
        h_ref[...] = (jax.nn.silu(gate) * up).astype(BF16)

    def side_casts():
        for src_ref, dst_ref in zip(src_refs, dst_refs):
            dst_ref[...] = src_ref[...].astype(BF16)

    @pl.when(j == 0)
    def _():
        xn_ref[...] = _rms(x_ref[...], g_ref[...]).astype(BF16)
        o_ref[...] = jnp.zeros_like(o_ref)
        hidden_chunk()
        side_casts()

    @pl.when((j > 0) & (j < last))
    def _():
        o_ref[...] += _dot(h_ref[...], wo_ref[...])
        hidden_chunk()
        side_casts()

    @pl.when(j == last)
    def _():
        o_ref[...] = x_ref[...] + 0.5 * (o_ref[...] + _dot(h_ref[...], wo_ref[...]))
        side_casts()


def _cast_rows(rows, steps):
    for rb in range(BF16_ROWS, rows + 1, BF16_ROWS):
        if rows % rb == 0 and rows // rb <= steps:
            return rb
    raise ValueError((rows, steps))


def _ffn(x, g, w_in, w_out, layer, k, tm, tf, casts=()):
    T, D = x.shape
    F = w_out.shape[2]
    nf = F // tf
    nj = nf + 1
    steps = (T // tm) * nj
    in_specs = [
        pl.BlockSpec((tm, D), lambda i, j: (i, 0)),
        pl.BlockSpec((None, None, 1, D), lambda i, j: (layer, k, 0, 0)),
        pl.BlockSpec((None, None, D, tf), lambda i, j: (0, 0, 0, jnp.minimum(j, nf - 1))),
        pl.BlockSpec((None, None, D, tf), lambda i, j: (0, 0, 0, jnp.minimum(j, nf - 1) + nf)),
        pl.BlockSpec((None, None, tf, D), lambda i, j: (0, 0, jnp.maximum(j - 1, 0), 0)),
    ]
    out_specs = [pl.BlockSpec((tm, D), lambda i, j: (i, 0))]
    out_shape = [jax.ShapeDtypeStruct((T, D), F32)]
    for arr, lead in casts:
        rows, cols = arr.shape[-2:]
        rb = _cast_rows(rows, steps)
        nblk = rows // rb
        squeezed = (None,) * len(lead)

        def src_map(i, j, lead=lead, nblk=nblk):
            return (*lead, jnp.minimum(i * nj + j, nblk - 1), 0)

        def dst_map(i, j, lead=lead, nblk=nblk):
            return (*((0,) * len(lead)), jnp.minimum(i * nj + j, nblk - 1), 0)

        in_specs.append(pl.BlockSpec((*squeezed, rb, cols), src_map))
        out_specs.append(pl.BlockSpec((*squeezed, rb, cols), dst_map))
        out_shape.append(jax.ShapeDtypeStruct((1,) * len(lead) + (rows, cols), BF16))
    outs = pl.pallas_call(
        functools.partial(_ffn_kernel, n_casts=len(casts)),
        grid=(T // tm, nj),
        in_specs=in_specs,
        out_specs=out_specs,
        out_shape=out_shape,
        scratch_shapes=[pltpu.VMEM((tm, D), BF16), pltpu.VMEM((tm, tf), BF16)],
        compiler_params=_params("parallel", "arbitrary"),
        name="ffn",
    )(x, g, w_in, w_in, w_out, *(arr for arr, _ in casts))
    return outs[0], outs[1:]


def _segment_major_perm(chunk, inverse):
    seg = chunk // SUBLANES
    i = lax.broadcasted_iota(jnp.int32, (chunk, chunk), 0)
    j = lax.broadcasted_iota(jnp.int32, (chunk, chunk), 1)
    src = (i % seg) * SUBLANES + i // seg if inverse else (i % SUBLANES) * seg + i // SUBLANES
    return jnp.where(j == src, 1.0, 0.0).astype(BF16)


def _permute_chunks(dst_ref, src, chunk, inverse):
    perm = _segment_major_perm(chunk, inverse)
    for c in range(src.shape[0] // chunk):
        rows = slice(c * chunk, (c + 1) * chunk)
        dst_ref[rows, :] = _dot(perm, src[rows, :]).astype(BF16)


def _inproj_kernel(x_ref, g_ref, w_ref, o_ref, xn_ref, *, chunk):
    @pl.when(pl.program_id(1) == 0)
    def _():
        xn = _rms(x_ref[...], g_ref[...]).astype(BF16)
        _permute_chunks(xn_ref, xn, chunk, inverse=False)

    o_ref[...] = _dot(xn_ref[...], w_ref[...])


def _inproj(x, g, w, layer, tm, tn, chunk):
    T, D = x.shape
    N = w.shape[2]
    assert tm % chunk == 0
    return pl.pallas_call(
        functools.partial(_inproj_kernel, chunk=chunk),
        grid=(T // tm, N // tn),
        in_specs=[
            pl.BlockSpec((tm, D), lambda i, j: (i, 0)),
            pl.BlockSpec((None, 1, D), lambda i, j: (layer, 0, 0)),
            pl.BlockSpec((None, D, tn), lambda i, j: (0, 0, j)),
        ],
        out_specs=pl.BlockSpec((tm, tn), lambda i, j: (i, j)),
        out_shape=jax.ShapeDtypeStruct((T, N), F32),
        scratch_shapes=[pltpu.VMEM((tm, D), BF16)],
        compiler_params=_params("parallel", "arbitrary"),
        name="inproj",
    )(x, g, w)


def _inproj_heads_kernel(x_ref, g_ref, w_ref, qkg_ref, o_ref, xn_ref, *, tiles_per_part):
    j = pl.program_id(1)

    @pl.when(j == 0)
    def _():
        xn_ref[...] = _rms(x_ref[...], g_ref[...]).astype(BF16)

    z = _dot(xn_ref[...], w_ref[...])
    part = j // tiles_per_part
    is_qk = (part < 3 * ATT_GROUPS) & (part % 3 < 2)
    gain = jnp.where(is_qk, qkg_ref[...], 1.0)
    for h in range(o_ref.shape[0]):
        zh = z[:, h * LANES:(h + 1) * LANES]
        inv = lax.rsqrt(jnp.mean(zh * zh, axis=-1, keepdims=True) + EPS)
        o_ref[h] = zh * jnp.where(is_qk, inv, 1.0) * gain


def _inproj_heads(x, g, w, qk_gain, layer, att_layer, tm, tn):
    T, D = x.shape
    N = w.shape[2]
    assert ATT_HEAD_DIM == LANES and tn % LANES == 0 and ATT_WIDTH % tn == 0
    tpp = ATT_WIDTH // tn

    def gain_map(i, j):
        part = j // tpp
        return (att_layer, jnp.minimum(part % 3, 1), jnp.minimum(part // 3, ATT_GROUPS - 1), 0, 0)

    kern = functools.partial(_inproj_heads_kernel, tiles_per_part=tpp)
    return pl.pallas_call(
        kern,
        grid=(T // tm, N // tn),
        in_specs=[
            pl.BlockSpec((tm, D), lambda i, j: (i, 0)),
            pl.BlockSpec((None, 1, D), lambda i, j: (layer, 0, 0)),
            pl.BlockSpec((None, D, tn), lambda i, j: (0, 0, j)),
            pl.BlockSpec((None, None, None, 1, LANES), gain_map),
        ],
        out_specs=pl.BlockSpec((tn // LANES, tm, LANES), lambda i, j: (j, i, 0)),
        out_shape=jax.ShapeDtypeStruct((N // LANES, T, LANES), F32),
        scratch_shapes=[pltpu.VMEM((tm, D), BF16)],
        compiler_params=_params("parallel", "arbitrary"),
        name="inproj_heads",
    )(x, g, w, qk_gain)


def _memkv_kernel(mem_ref, g_ref, w_ref, kg_ref, k_ref, v_ref, *, heads, hd):
    mem_n = _rms(mem_ref[...], g_ref[...]).astype(BF16)
    kv = _dot(mem_n, w_ref[...])
    for h in range(heads):
        sl = slice(h * hd, (h + 1) * hd)
        k_ref[:, sl] = _rms(kv[:, sl], kg_ref[...]).astype(BF16)
    v_ref[...] = kv[:, heads * hd:].astype(BF16)


def _memkv(mem, g, w_kv, qk_gain, layer):
    B, M, D = mem.shape
    mw = w_kv.shape[2] // 2
    hd = mw // MEM_HEADS
    kern = functools.partial(_memkv_kernel, heads=MEM_HEADS, hd=hd)
    return pl.pallas_call(
        kern,
        grid=(B,),
        in_specs=[
            pl.BlockSpec((None, M, D), lambda b: (b, 0, 0)),
            pl.BlockSpec((None, 1, D), lambda b: (layer, 0, 0)),
            pl.BlockSpec((None, D, 2 * mw), lambda b: (0, 0, 0)),
            pl.BlockSpec((None, None, 1, hd), lambda b: (layer, 1, 0, 0)),
        ],
        out_specs=[pl.BlockSpec((None, M, mw), lambda b: (b, 0, 0))] * 2,
        out_shape=[jax.ShapeDtypeStruct((B, M, mw), BF16)] * 2,
        compiler_params=_params("parallel"),
        name="memkv",
    )(mem, g, w_kv, qk_gain)


def _memattn_kernel(q_ref, qg_ref, k_ref, v_ref, o_ref, *, heads, hd, head_major):
    scale = hd ** -0.5
    per = hd // LANES
    for h in range(heads):
        sl = slice(h * hd, (h + 1) * hd)
        if head_major:
            qh = jnp.concatenate([q_ref[h * per + c] for c in range(per)], axis=1)
        else:
            qh = q_ref[:, sl]
        q = _rms(qh, qg_ref[...]).astype(BF16)
        s = _dot_nt(q, k_ref[:, sl]) * scale
        e = jnp.exp(s - jnp.max(s, axis=-1, keepdims=True))
        p = e / jnp.sum(e, axis=-1, keepdims=True)
        o_ref[:, sl] = _dot(p.astype(BF16), v_ref[:, sl]).astype(o_ref.dtype)


def _memattn(z, q_col, mk, mv, qk_gain, layer, S, tm):
    head_major = z.ndim == 3
    T = z.shape[1] if head_major else z.shape[0]
    _, M, mw = mk.shape
    hd = mw // MEM_HEADS
    qb = q_col // mw
    assert qb * mw == q_col and hd % LANES == 0
    per_b = S // tm
    if head_major:
        q_spec = pl.BlockSpec((mw // LANES, tm, LANES), lambda i: (qb, i, 0))
    else:
        q_spec = pl.BlockSpec((tm, mw), lambda i: (i, qb))
    kern = functools.partial(_memattn_kernel, heads=MEM_HEADS, hd=hd, head_major=head_major)
    return pl.pallas_call(
        kern,
        grid=(T // tm,),
        in_specs=[
            q_spec,
            pl.BlockSpec((None, None, 1, hd), lambda i: (layer, 0, 0, 0)),
            pl.BlockSpec((None, M, mw), lambda i: (i // per_b, 0, 0)),
            pl.BlockSpec((None, M, mw), lambda i: (i // per_b, 0, 0)),
        ],
        out_specs=pl.BlockSpec((tm, mw), lambda i: (i, 0)),
        out_shape=jax.ShapeDtypeStruct((T, mw), BF16),
        compiler_params=_params("parallel"),
        name="memattn",
    )(z, qk_gain, mk, mv)


def _outproj_kernel(ya_ref, ym_ref, wa_ref, wm_ref, x_ref, o_ref, ya_seq, ym_seq, *, chunk):
    @pl.when(pl.program_id(1) == 0)
    def _():
        _permute_chunks(ya_seq, ya_ref[...], chunk, inverse=True)
        _permute_chunks(ym_seq, ym_ref[...], chunk, inverse=True)

    o_ref[...] = x_ref[...] + _dot(ya_seq[...], wa_ref[...]) + _dot(ym_seq[...], wm_ref[...])


def _outproj_heads_kernel(ya_ref, ym_ref, wa_ref, wm_ref, x_ref, o_ref, cat_ref):
    @pl.when(pl.program_id(1) == 0)
    def _():
        for h in range(ya_ref.shape[0]):
            cat_ref[:, h * LANES:(h + 1) * LANES] = ya_ref[h]

    o_ref[...] = x_ref[...] + _dot(cat_ref[...], wa_ref[...]) + _dot(ym_ref[...], wm_ref[...])


def _outproj(x, ya, ym, w, tm, tn, chunk=None):
    T, D = x.shape
    head_major = ya.ndim == 3
    assert head_major == (chunk is None)
    ka = ya.shape[0] * LANES if head_major else ya.shape[1]
    km = ym.shape[1]
    mb = ka // km
    assert mb * km == ka
    if head_major:
        ya_spec = pl.BlockSpec((ka // LANES, tm, LANES), lambda i, j: (0, i, 0))
        kern, scratch = _outproj_heads_kernel, [pltpu.VMEM((tm, ka), BF16)]
    else:
        assert tm % chunk == 0
        ya_spec = pl.BlockSpec((tm, ka), lambda i, j: (i, 0))
        kern = functools.partial(_outproj_kernel, chunk=chunk)
        scratch = [pltpu.VMEM((tm, ka), BF16), pltpu.VMEM((tm, km), BF16)]
    return pl.pallas_call(
        kern,
        grid=(T // tm, D // tn),
        in_specs=[
            ya_spec,
            pl.BlockSpec((tm, km), lambda i, j: (i, 0)),
            pl.BlockSpec((None, ka, tn), lambda i, j: (0, 0, j)),
            pl.BlockSpec((None, km, tn), lambda i, j: (0, mb, j)),
            pl.BlockSpec((tm, tn), lambda i, j: (i, j)),
        ],
        out_specs=pl.BlockSpec((tm, tn), lambda i, j: (i, j)),
        out_shape=jax.ShapeDtypeStruct((T, D), F32),
        scratch_shapes=scratch,
        compiler_params=_params("parallel", "arbitrary"),
        name="outproj",
    )(ya, ym, w, w, x)


def _rec_kernel(*refs, reverse, final, ts, bw):
    if final:
        (prev_ref, main_ref, next_ref, cw_ref, cb_ref, gw_ref, gb_ref, lam_ref,
         gate_ref, hb_ref, o_ref, a_ref, u_ref, h_ref) = refs
    else:
        (prev_ref, main_ref, next_ref, cw_ref, cb_ref, gw_ref, gb_ref, lam_ref,
         o_ref, a_ref, u_ref, h_ref) = refs
    c = pl.program_id(1)
    nc = pl.num_programs(1)
    chunk = nc - 1 - c if reverse else c
    seg = ts // SUBLANES
    W = main_ref.shape[1]
    assert CONV_WIDTH == 4 and CONV_LEFT == 2

    @pl.when(c == 0)
    def _():
        h_ref[...] = jnp.zeros_like(h_ref)

    main = main_ref[...]
    prev = jnp.where(chunk > 0, prev_ref[...], 0.0)
    nxt = jnp.where(chunk < nc - 1, next_ref[...], 0.0)
    sub = lax.broadcasted_iota(jnp.int32, (SUBLANES, W), 0)

    def from_prev_segment(tile, fill):
        return jnp.where(sub == 0, fill, pltpu.roll(tile, 1, axis=0))

    def from_next_segment(tile, fill):
        return jnp.where(sub == SUBLANES - 1, fill, pltpu.roll(tile, SUBLANES - 1, axis=0))

    back1 = from_prev_segment(main[ts - SUBLANES:], prev[2 * SUBLANES - 1:])
    back2 = from_prev_segment(main[ts - 2 * SUBLANES:ts - SUBLANES], prev[SUBLANES - 1:SUBLANES])
    ahead1 = from_next_segment(main[:SUBLANES], nxt[0:1])
    taps = (jnp.concatenate([back2, back1, main[:ts - 2 * SUBLANES]], axis=0),
            jnp.concatenate([back1, main[:ts - SUBLANES]], axis=0),
            main,
            jnp.concatenate([main[SUBLANES:], ahead1], axis=0))
    xc = cb_ref[...]
    for k in range(CONV_WIDTH):
        xc = xc + cw_ref[k:k + 1, :] * taps[k]

    log_a_scale = (-0.5 * LRU_C) * jax.nn.softplus(-lam_ref[...])
    xh = 0.5 * xc
    xhb = xh.astype(BF16)
    gbh = 0.5 * gb_ref[...]
    for n in range(LRU_BLOCKS):
        sl = slice(n * bw, (n + 1) * bw)
        tr = jnp.tanh(_dot(xhb[:, sl], gw_ref[0, n]) + gbh[0:1, sl])
        ti = jnp.tanh(_dot(xhb[:, sl], gw_ref[1, n]) + gbh[1:2, sl])
        a = jnp.exp((tr + 1.0) * log_a_scale[:, sl])
        y = 1.0 - a * a
        a_ref[:, sl] = a
        u_ref[:, sl] = (y * lax.rsqrt(jnp.maximum(y, 1e-30))) * ((ti + 1.0) * xh[:, sl])

    def step(t, carry):
        h, prod = carry
        k = seg - 1 - t if reverse else t
        rows = pl.ds(pl.multiple_of(k * SUBLANES, SUBLANES), SUBLANES)
        a = a_ref[rows, :]
        h = a * h + u_ref[rows, :]
        prod = a * prod
        u_ref[rows, :] = h
        a_ref[rows, :] = prod
        return h, prod

    h_end, a_end = lax.fori_loop(0, seg, step, (jnp.zeros((SUBLANES, W), F32), jnp.ones((SUBLANES, W), F32)),
                                 unroll=2)

    carry = h_ref[...]
    h_in = jnp.zeros((SUBLANES, W), F32)
    for s in (range(SUBLANES - 1, -1, -1) if reverse else range(SUBLANES)):
        h_in = jnp.where(sub == s, carry, h_in)
        carry = h_end[s:s + 1] + a_end[s:s + 1] * carry
    h_ref[...] = carry

    hs = u_ref[...].reshape(seg, SUBLANES, W) + a_ref[...].reshape(seg, SUBLANES, W) * h_in[None]
    hs = hs.reshape(ts, W)
    if final:
        o_ref[...] = ((hs + hb_ref[...]) * jax.nn.gelu(gate_ref[...])).astype(o_ref.dtype)
    else:
        o_ref[...] = hs


def _rec_scan(z, conv_w, conv_b, gate_w, gate_b, lam, j, direction, h_other, S, ts):
    T = z.shape[0]
    W = conv_w.shape[2]
    bw = W // LRU_BLOCKS
    nc = S // ts
    reverse = direction == 1
    final = h_other is not None
    assert ts % (2 * SUBLANES) == 0

    def chunk_of(c):
        return nc - 1 - c if reverse else c

    def main_map(b, c):
        return (b * nc + chunk_of(c), 1)

    def prev_map(b, c):
        return (jnp.maximum((b * nc + chunk_of(c)) * (ts // (2 * SUBLANES)) - 1, 0), 1)

    def next_map(b, c):
        return (jnp.minimum((b * nc + chunk_of(c) + 1) * (ts // SUBLANES), T // SUBLANES - 1), 1)

    in_specs = [
        pl.BlockSpec((2 * SUBLANES, W), prev_map),
        pl.BlockSpec((ts, W), main_map),
        pl.BlockSpec((SUBLANES, W), next_map),
        pl.BlockSpec((None, CONV_WIDTH, W), lambda b, c: (j, 0, 0)),
        pl.BlockSpec((None, 1, W), lambda b, c: (j, 0, 0)),
        pl.BlockSpec((None, None, 2, LRU_BLOCKS, bw, bw), lambda b, c: (0, direction, 0, 0, 0, 0)),
        pl.BlockSpec((None, None, 2, W), lambda b, c: (j, direction, 0, 0)),
        pl.BlockSpec((None, None, 1, W), lambda b, c: (j, direction, 0, 0)),
    ]
    args = [z, z, z, conv_w, conv_b, gate_w, gate_b, lam]
    if final:
        in_specs += [
            pl.BlockSpec((ts, W), lambda b, c: (b * nc + chunk_of(c), 0)),
            pl.BlockSpec((ts, W), lambda b, c: (b * nc + chunk_of(c), 0)),
        ]
        args += [z, h_other]
    kern = functools.partial(_rec_kernel, reverse=reverse, final=final, ts=ts, bw=bw)
    return pl.pallas_call(
        kern,
        grid=(T // S, nc),
        in_specs=in_specs,
        out_specs=pl.BlockSpec((ts, W), lambda b, c: (b * nc + chunk_of(c), 0)),
        out_shape=jax.ShapeDtypeStruct((T, W), BF16 if final else F32),
        scratch_shapes=[
            pltpu.VMEM((ts, W), F32),
            pltpu.VMEM((ts, W), F32),
            pltpu.VMEM((1, W), F32),
        ],
        compiler_params=_params("parallel", "arbitrary"),
        name="rec_fwd" if final else "rec_bwd",
    )(*args)


ATT_TQ = 128
ATT_INTERLEAVE = 16


def _alibi_slopes():
    n = ATT_GROUPS * ATT_HEADS
    s = [2.0 ** (-8.0 * (i + 1.0) / n) for i in range(n)]
    return np.asarray(s, np.float32).reshape(ATT_GROUPS, ATT_HEADS)


def _rows(start, size, stride):
    if stride > 1:
        return pl.ds(start, size, stride=stride)
    return pl.ds(start if isinstance(start, int) else pl.multiple_of(start, SUBLANES), size)


def _attn_kernel(slopes_ref, *refs, span, halves):
    o_ref, acc_o, acc_l = refs[7 * ATT_GROUPS:]
    head = pl.program_id(1)
    blk = pl.program_id(2)
    nblk = pl.num_programs(2)
    tq, hd = ATT_TQ, ATT_HEAD_DIM
    half = halves[0]
    nk = tq + 2 * half
    row = lax.broadcasted_iota(jnp.int32, (tq, nk), 0)
    col = lax.broadcasted_iota(jnp.int32, (tq, nk), 1)
    rel = jnp.abs(col - half - row)
    in_band = rel <= half
    rel_f = rel.astype(F32)
    col1 = lax.broadcasted_iota(jnp.int32, (1, nk), 1)
    lo_mask = jnp.where((col1 < half) & (blk == 0), NEG_INF, 0.0)
    hi_mask = jnp.where((col1 >= nk - half) & (blk == nblk - 1), NEG_INF, 0.0)
    scale = hd ** -0.5

    for g, (_, d) in enumerate(ATT_PATTERNS):
        assert halves[g] == half
        q_ref, kp_ref, kc_ref, kn_ref, vp_ref, vc_ref, vn_ref = refs[7 * g:7 * g + 7]
        nu = span // (tq * d)
        slope = slopes_ref[g, head]
        bias_mid = jnp.where(in_band, -slope * (d * rel_f), NEG_INF)
        biases = {"mid": bias_mid, "first": bias_mid + lo_mask, "last": bias_mid + hi_mask,
                  "only": bias_mid + lo_mask + hi_mask}

        def tiles(specs, g=g, d=d, q_ref=q_ref, kp_ref=kp_ref, kc_ref=kc_ref, kn_ref=kn_ref,
                  vp_ref=vp_ref, vc_ref=vc_ref, vn_ref=vn_ref, biases=biases):
            done = []
            for r, u, kind in specs:
                qrows = _rows(u * (tq * d) + r, tq, d)
                lo = (u * tq - half) * d + r
                if kind == "only":
                    pieces = [(kp_ref, vp_ref, _rows(r, half, d)), (kc_ref, vc_ref, _rows(r, tq, d)),
                              (kn_ref, vn_ref, _rows(r, half, d))]
                elif kind == "first":
                    pieces = [(kp_ref, vp_ref, _rows(r, half, d)), (kc_ref, vc_ref, _rows(r, tq + half, d))]
                elif kind == "last":
                    pieces = [(kc_ref, vc_ref, _rows(lo, tq + half, d)), (kn_ref, vn_ref, _rows(r, half, d))]
                else:
                    pieces = [(kc_ref, vc_ref, _rows(lo, nk, d))]
                k = jnp.concatenate([kr[rows, :] for kr, _, rows in pieces], axis=0).astype(BF16)
                v = jnp.concatenate([vr[rows, :] for _, vr, rows in pieces], axis=0).astype(BF16)
                q = q_ref[qrows, :].astype(BF16)
                s = _dot_nt(q, k) * scale + biases[kind]
                m = jnp.max(s, axis=-1, keepdims=True)
                e = jnp.exp(s - m)
                den = jnp.sum(e, axis=-1, keepdims=True)
                o = _dot((e / den).astype(BF16), v)
                lse = jnp.broadcast_to(m + jnp.log(den), (tq, hd))
                prev = (acc_o[qrows, :], acc_l[qrows, :]) if g > 0 else None
                done.append((qrows, o, lse, prev))
            for qrows, o, lse, prev in done:
                if prev is not None:
                    o_acc, lse_acc = prev
                    top = jnp.maximum(lse_acc, lse)
                    w_acc = jnp.exp(lse_acc - top)
                    w_cur = jnp.exp(lse - top)
                    tot = w_acc + w_cur
                    o = (w_acc * o_acc + w_cur * o) / tot
                    lse = top + jnp.log(tot)
                acc_o[qrows, :] = o
                if g < ATT_GROUPS - 1:
                    acc_l[qrows, :] = lse

        def kind_of(u, nu=nu):
            return "only" if nu == 1 else "first" if u == 0 else "last" if u == nu - 1 else "mid"

        ilv = ATT_INTERLEAVE
        if nu >= ilv:
            assert nu % ilv == 0
            nchunk = nu // ilv

            def per_class(r, nu=nu, nchunk=nchunk, tiles=tiles, kind_of=kind_of):
                tiles([(r, u, kind_of(u)) for u in range(ilv)])
                if nchunk > 2:
                    def mid(c, carry):
                        tiles([(r, c * ilv + t, "mid") for t in range(ilv)])
                        return carry
                    lax.fori_loop(1, nchunk - 1, mid, 0)
                if nchunk > 1:
                    tiles([(r, u, kind_of(u)) for u in range(nu - ilv, nu)])

            if d == 1:
                per_class(0)
            else:
                def classes(r, carry, per_class=per_class):
                    per_class(r)
                    return carry
                lax.fori_loop(0, d, classes, 0)
        else:
            per_blk = ilv // nu
            assert per_blk * nu == ilv and d % per_blk == 0

            def classes(rb, carry, nu=nu, per_blk=per_blk, tiles=tiles, kind_of=kind_of):
                tiles([(rb * per_blk + t, u, kind_of(u)) for t in range(per_blk) for u in range(nu)])
                return carry
            lax.fori_loop(0, d // per_blk, classes, 0)

    o_ref[...] = acc_o[...].astype(o_ref.dtype)


def _attention(zh, B, S, span):
    T = zh.shape[1]
    nb = S // span
    halves = tuple(w // (2 * d) for w, d in ATT_PATTERNS)
    in_specs = [pl.BlockSpec(memory_space=pltpu.SMEM)]
    for g, (_, d) in enumerate(ATT_PATTERNS):
        halo = halves[g] * d
        assert span % (ATT_TQ * d) == 0 and span % halo == 0 and S % span == 0
        ratio = span // halo

        def slab(p, g=g):
            return (3 * g + p) * ATT_HEADS

        def cur(p, slab=slab):
            return pl.BlockSpec((None, span, LANES), lambda b, h, i: (slab(p) + h, b * nb + i, 0))

        def before(p, slab=slab, halo=halo, ratio=ratio):
            return pl.BlockSpec((None, halo, LANES),
                                lambda b, h, i: (slab(p) + h, jnp.maximum((b * nb + i) * ratio - 1, 0), 0))

        def after(p, slab=slab, halo=halo, ratio=ratio):
            return pl.BlockSpec((None, halo, LANES),
                                lambda b, h, i: (slab(p) + h, jnp.minimum((b * nb + i + 1) * ratio, T // halo - 1), 0))

        in_specs += [cur(0), before(1), cur(1), after(1), before(2), cur(2), after(2)]
    kern = functools.partial(_attn_kernel, span=span, halves=halves)
    return pl.pallas_call(
        kern,
        grid=(B, ATT_HEADS, nb),
        in_specs=in_specs,
        out_specs=pl.BlockSpec((None, span, LANES), lambda b, h, i: (h, b * nb + i, 0)),
        out_shape=jax.ShapeDtypeStruct((ATT_HEADS, T, LANES), BF16),
        scratch_shapes=[pltpu.VMEM((span, LANES), F32), pltpu.VMEM((span, LANES), F32)],
        compiler_params=_params("parallel", "parallel", "arbitrary"),
        name="attn",
    )(jnp.asarray(_alibi_slopes()), *([zh] * (7 * ATT_GROUPS)))


def kernel(x, mem, ffn_norm, ffn_w_in, ffn_w_out, mix_norm, mem_norm, mem_w_kv, mem_qk_gain,
           rec_w_in, rec_conv_w, rec_conv_b, rec_gate_w, rec_gate_b, rec_lambda, rec_w_out,
           att_w_in, att_qk_gain, att_w_out):
    B, S, D = x.shape
    T = B * S
    depth = ffn_norm.shape[0]
    W = rec_conv_w.shape[2]

    tm_ffn = min(1024, S)
    tf = min(512, ffn_w_out.shape[2])
    tm_proj = min(1024, S)
    tn_proj = min(1024, D)
    ts_rec = min(256, S)
    span = ATT_TQ * max(d for _, d in ATT_PATTERNS)

    ffn_norm = ffn_norm[:, :, None, :]
    mix_norm = mix_norm[:, None, :]
    mem_norm = mem_norm[:, None, :]
    mem_qk_gain = mem_qk_gain[:, :, None, :]
    rec_conv_b = rec_conv_b[:, None, :]
    rec_gate_b = rec_gate_b.reshape(rec_gate_b.shape[0], 2, 2, W)
    rec_lambda = rec_lambda[:, :, None, :]
    att_qk_gain = att_qk_gain[:, :, :, None, :]
    gate_w_rows = rec_gate_w.reshape(rec_gate_w.shape[0], -1, rec_gate_w.shape[-1])

    def mixer_casts(layer):
        j = layer // 2
        if layer % 2 == 0:
            return [(rec_w_in, (j,)), (mem_w_kv, (layer,))], [(rec_w_out, (j,)), (gate_w_rows, (j,))]
        return [(att_w_in, (j,)), (mem_w_kv, (layer,))], [(att_w_out, (j,))]

    def cast_now(arr, lead):
        return arr[tuple(slice(i, i + 1) for i in lead)].astype(BF16)

    ffn_w = [cast_now(ffn_w_in, (0, 0)), cast_now(ffn_w_out, (0, 0))]
    mix_w = [cast_now(*c) for part in mixer_casts(0) for c in part]

    x = x.reshape(T, D)
    for layer in range(depth):
        more = layer + 1 < depth
        early, late = mixer_casts(layer + 1) if more else ([], [])
        x, done = _ffn(x, ffn_norm, *ffn_w, layer, 0, tm_ffn, tf,
                       [(ffn_w_in, (layer, 1)), (ffn_w_out, (layer, 1))] + early)
        ffn_w, next_mix_w = done[:2], list(done[2:])
        w_in, w_kv, w_out, *w_gate = mix_w
        mk, mv = _memkv(mem, mem_norm, w_kv, mem_qk_gain, layer)
        j = layer // 2
        if layer % 2 == 0:
            gate_w = w_gate[0].reshape((1,) + rec_gate_w.shape[1:])
            z = _inproj(x, mix_norm, w_in, layer, tm_proj, tn_proj, ts_rec)
            hb = _rec_scan(z, rec_conv_w, rec_conv_b, gate_w, rec_gate_b, rec_lambda, j, 1, None, S, ts_rec)
            ya = _rec_scan(z, rec_conv_w, rec_conv_b, gate_w, rec_gate_b, rec_lambda, j, 0, hb, S, ts_rec)
            ym = _memattn(z, 2 * W, mk, mv, mem_qk_gain, layer, S, tm_proj)
            x = _outproj(x, ya, ym, w_out, tm_proj, tn_proj, ts_rec)
        else:
            zh = _inproj_heads(x, mix_norm, w_in, att_qk_gain, layer, j, tm_proj, tn_proj)
            ya = _attention(zh, B, S, span)
            ym = _memattn(zh, ATT_GROUPS * 3 * ATT_WIDTH, mk, mv, mem_qk_gain, layer, S, tm_proj)
            x = _outproj(x, ya, ym, w_out, tm_proj, tn_proj)
        x, done = _ffn(x, ffn_norm, *ffn_w, layer, 1, tm_ffn, tf,
                       [(ffn_w_in, (layer + 1, 0)), (ffn_w_out, (layer + 1, 0))] + late if more else [])
        ffn_w, mix_w = done[:2], next_mix_w + list(done[2:])
    return x.reshape(B, S, D)
```

```python
import functools

import jax
import jax.numpy as jnp
import numpy as np
from jax import lax
from jax.experimental import pallas as pl
from jax.experimental.pallas import tpu as pltpu

F32 = jnp.float32
BF16 = jnp.bfloat16

EPS = 1e-6
NEG_INF = -1e30
LRU_C = 8.0
LRU_BLOCKS = 8
CONV_WIDTH = 4
CONV_LEFT = CONV_WIDTH // 2
ATT_PATTERNS = ((128, 1), (512, 4), (2048, 16))
ATT_GROUPS = len(ATT_PATTERNS)
ATT_HEADS = 8
ATT_HEAD_DIM = 128
ATT_WIDTH = ATT_HEADS * ATT_HEAD_DIM
MEM_HEADS = 4
ATT_CHUNK = 256
SUBLANES = 8
LANES = 128
BF16_ROWS = 16
VMEM_LIMIT_BYTES = 60 * 1024 * 1024


def _params(*semantics):
    return pltpu.CompilerParams(dimension_semantics=semantics, vmem_limit_bytes=VMEM_LIMIT_BYTES)


def _rms(x, g):
    return x * lax.rsqrt(jnp.mean(x * x, axis=-1, keepdims=True) + EPS) * g


def _dot(a, b):
    return jnp.dot(a, b, preferred_element_type=F32)


def _dot_nt(a, b):
    return lax.dot_general(a, b, (((1,), (1,)), ((), ())), preferred_element_type=F32)


def _ffn_kernel(*refs, n_casts):
    x_ref, g_ref, wg_ref, wu_ref, wo_ref = refs[:5]
    src_refs = refs[5:5 + n_casts]
    o_ref = refs[5 + n_casts]
    dst_refs = refs[6 + n_casts:6 + 2 * n_casts]
    xn_ref = refs[6 + 2 * n_casts]
    j = pl.program_id(1)

    @pl.when(j == 0)
    def _():
        xn_ref[...] = _rms(x_ref[...], g_ref[...]).astype(BF16)
        o_ref[...] = jnp.zeros_like(o_ref)

    xn = xn_ref[...]
    gate = _dot(xn, wg_ref[...])
    up = _dot(xn, wu_ref[...])
    h = (jax.nn.silu(gate) * up).astype(BF16)
    o_ref[...] += _dot(h, wo_ref[...])

    for src_ref, dst_ref in zip(src_refs, dst_refs):
        dst_ref[...] = src_ref[...].astype(BF16)

    @pl.when(j == pl.num_programs(1) - 1)
    def _():
        o_ref[...] = x_ref[...] + 0.5 * o_ref[...]


def _cast_rows(rows, steps):
    for rb in range(BF16_ROWS, rows + 1, BF16_ROWS):
        if rows % rb == 0 and rows // rb <= steps:
            return rb
    raise ValueError((rows, steps))


def _ffn(x, g, w_in, w_out, layer, k, tm, tf, casts=()):
    T, D = x.shape
    F = w_out.shape[2]
    nf = F // tf
    steps = (T // tm) * nf
    in_specs = [
        pl.BlockSpec((tm, D), lambda i, j: (i, 0)),
        pl.BlockSpec((None, None, 1, D), lambda i, j: (layer, k, 0, 0)),
        pl.BlockSpec((None, None, D, tf), lambda i, j: (0, 0, 0, j)),
        pl.BlockSpec((None, None, D, tf), lambda i, j: (0, 0, 0, j + nf)),
        pl.BlockSpec((None, None, tf, D), lambda i, j: (0, 0, j, 0)),
    ]
    out_specs = [pl.BlockSpec((tm, D), lambda i, j: (i, 0))]
    out_shape = [jax.ShapeDtypeStruct((T, D), F32)]
    for arr, lead in casts:
        rows, cols = arr.shape[-2:]
        rb = _cast_rows(rows, steps)
        nblk = rows // rb
        squeezed = (None,) * len(lead)

        def src_map(i, j, lead=lead, nblk=nblk):
            return (*lead, jnp.minimum(i * nf + j, nblk - 1), 0)

        def dst_map(i, j, lead=lead, nblk=nblk):
            return (*((0,) * len(lead)), jnp.minimum(i * nf + j, nblk - 1), 0)

        in_specs.append(pl.BlockSpec((*squeezed, rb, cols), src_map))
        out_specs.append(pl.BlockSpec((*squeezed, rb, cols), dst_map))
        out_shape.append(jax.ShapeDtypeStruct((1,) * len(lead) + (rows, cols), BF16))
    outs = pl.pallas_call(
        functools.partial(_ffn_kernel, n_casts=len(casts)),
        grid=(T // tm, nf),
        in_specs=in_specs,
        out_specs=out_specs,
        out_shape=out_shape,
        scratch_shapes=[pltpu.VMEM((tm, D), BF16)],
        compiler_params=_params("parallel", "arbitrary"),
        name="ffn",
    )(x, g, w_in, w_in, w_out, *(arr for arr, _ in casts))
    return outs[0], outs[1:]


def _segment_major_perm(chunk, inverse):
    seg = chunk // SUBLANES
    i = lax.broadcasted_iota(jnp.int32, (chunk, chunk), 0)
    j = lax.broadcasted_iota(jnp.int32, (chunk, chunk), 1)
    src = (i % seg) * SUBLANES + i // seg if inverse else (i % SUBLANES) * seg + i // SUBLANES
    return jnp.where(j == src, 1.0, 0.0).astype(BF16)


def _permute_chunks(dst_ref, src, chunk, inverse):
    perm = _segment_major_perm(chunk, inverse)
    for c in range(src.shape[0] // chunk):
        rows = slice(c * chunk, (c + 1) * chunk)
        dst_ref[rows, :] = _dot(perm, src[rows, :]).astype(BF16)


def _inproj_kernel(x_ref, g_ref, w_ref, o_ref, xn_ref, *, chunk):
    @pl.when(pl.program_id(1) == 0)
    def _():
        xn = _rms(x_ref[...], g_ref[...]).astype(BF16)
        _permute_chunks(xn_ref, xn, chunk, inverse=False)

    o_ref[...] = _dot(xn_ref[...], w_ref[...])


def _inproj(x, g, w, layer, tm, tn, chunk):
    T, D = x.shape
    N = w.shape[2]
    assert tm % chunk == 0
    return pl.pallas_call(
        functools.partial(_inproj_kernel, chunk=chunk),
        grid=(T // tm, N // tn),
        in_specs=[
            pl.BlockSpec((tm, D), lambda i, j: (i, 0)),
            pl.BlockSpec((None, 1, D), lambda i, j: (layer, 0, 0)),
            pl.BlockSpec((None, D, tn), lambda i, j: (0, 0, j)),
        ],
        out_specs=pl.BlockSpec((tm, tn), lambda i, j: (i, j)),
        out_shape=jax.ShapeDtypeStruct((T, N), F32),
        scratch_shapes=[pltpu.VMEM((tm, D), BF16)],
        compiler_params=_params("parallel", "arbitrary"),
        name="inproj",
    )(x, g, w)


def _class_major_perm(chunk, d):
    per = chunk // d
    i = lax.broadcasted_iota(jnp.int32, (chunk, chunk), 0)
    j = lax.broadcasted_iota(jnp.int32, (chunk, chunk), 1)
    return jnp.where(j == (i % per) * d + i // per, 1.0, 0.0).astype(BF16)


def _inproj_heads_kernel(x_ref, g_ref, w_ref, qkg_ref, o_ref, qm_ref, xn_ref):
    j = pl.program_id(1)

    @pl.when(j == 0)
    def _():
        xn = _rms(x_ref[...], g_ref[...]).astype(BF16)
        for grp, (_, d) in enumerate(ATT_PATTERNS):
            if d == 1:
                xn_ref[grp] = xn
            else:
                perm = _class_major_perm(ATT_CHUNK, d)
                for c in range(xn.shape[0] // ATT_CHUNK):
                    rows = slice(c * ATT_CHUNK, (c + 1) * ATT_CHUNK)
                    xn_ref[grp, rows, :] = _dot(perm, xn[rows, :]).astype(BF16)

    grp = jnp.where(j < 3 * ATT_GROUPS, j // 3, 0)
    z = _dot(xn_ref[grp], w_ref[...])
    is_qk = (j < 3 * ATT_GROUPS) & (j % 3 < 2)
    gain = jnp.where(is_qk, qkg_ref[...], 1.0)
    for h in range(o_ref.shape[0]):
        zh = z[:, h * LANES:(h + 1) * LANES]
        inv = lax.rsqrt(jnp.mean(zh * zh, axis=-1, keepdims=True) + EPS)
        zh = zh * jnp.where(is_qk, inv, 1.0) * gain
        o_ref[h] = zh.astype(o_ref.dtype)
        qm_ref[h] = zh


def _inproj_heads(x, g, w, qk_gain, layer, att_layer, tm):
    T, D = x.shape
    N = w.shape[2]
    tn = ATT_WIDTH
    assert ATT_HEAD_DIM == LANES and N % tn == 0 and tm % ATT_CHUNK == 0
    assert all(ATT_CHUNK % d == 0 for _, d in ATT_PATTERNS)

    def gain_map(i, j):
        return (att_layer, jnp.minimum(j % 3, 1), jnp.minimum(j // 3, ATT_GROUPS - 1), 0, 0)

    return pl.pallas_call(
        _inproj_heads_kernel,
        grid=(T // tm, N // tn),
        in_specs=[
            pl.BlockSpec((tm, D), lambda i, j: (i, 0)),
            pl.BlockSpec((None, 1, D), lambda i, j: (layer, 0, 0)),
            pl.BlockSpec((None, D, tn), lambda i, j: (0, 0, j)),
            pl.BlockSpec((None, None, None, 1, LANES), gain_map),
        ],
        out_specs=[pl.BlockSpec((tn // LANES, tm, LANES), lambda i, j: (j, i, 0)),
                   pl.BlockSpec((tn // LANES, tm, LANES), lambda i, j: (0, i, 0))],
        out_shape=[jax.ShapeDtypeStruct((N // LANES, T, LANES), BF16),
                   jax.ShapeDtypeStruct((tn // LANES, T, LANES), F32)],
        scratch_shapes=[pltpu.VMEM((ATT_GROUPS, tm, D), BF16)],
        compiler_params=_params("parallel", "arbitrary"),
        name="inproj_heads",
    )(x, g, w, qk_gain)


def _memkv_kernel(mem_ref, g_ref, w_ref, kg_ref, k_ref, v_ref, *, heads, hd):
    mem_n = _rms(mem_ref[...], g_ref[...]).astype(BF16)
    kv = _dot(mem_n, w_ref[...])
    for h in range(heads):
        sl = slice(h * hd, (h + 1) * hd)
        k_ref[:, sl] = _rms(kv[:, sl], kg_ref[...]).astype(BF16)
    v_ref[...] = kv[:, heads * hd:].astype(BF16)


def _memkv(mem, g, w_kv, qk_gain, layer):
    B, M, D = mem.shape
    mw = w_kv.shape[2] // 2
    hd = mw // MEM_HEADS
    kern = functools.partial(_memkv_kernel, heads=MEM_HEADS, hd=hd)
    return pl.pallas_call(
        kern,
        grid=(B,),
        in_specs=[
            pl.BlockSpec((None, M, D), lambda b: (b, 0, 0)),
            pl.BlockSpec((None, 1, D), lambda b: (layer, 0, 0)),
            pl.BlockSpec((None, D, 2 * mw), lambda b: (0, 0, 0)),
            pl.BlockSpec((None, None, 1, hd), lambda b: (layer, 1, 0, 0)),
        ],
        out_specs=[pl.BlockSpec((None, M, mw), lambda b: (b, 0, 0))] * 2,
        out_shape=[jax.ShapeDtypeStruct((B, M, mw), BF16)] * 2,
        compiler_params=_params("parallel"),
        name="memkv",
    )(mem, g, w_kv, qk_gain)


def _memattn_kernel(q_ref, qg_ref, k_ref, v_ref, o_ref, *, heads, hd, head_major):
    scale = hd ** -0.5
    per = hd // LANES
    for h in range(heads):
        sl = slice(h * hd, (h + 1) * hd)
        if head_major:
            qh = jnp.concatenate([q_ref[h * per + c] for c in range(per)], axis=1)
        else:
            qh = q_ref[:, sl]
        q = _rms(qh, qg_ref[...]).astype(BF16)
        s = _dot_nt(q, k_ref[:, sl]) * scale
        e = jnp.exp(s - jnp.max(s, axis=-1, keepdims=True))
        p = e / jnp.sum(e, axis=-1, keepdims=True)
        o_ref[:, sl] = _dot(p.astype(BF16), v_ref[:, sl]).astype(o_ref.dtype)


def _memattn(z, q_col, mk, mv, qk_gain, layer, S, tm):
    head_major = z.ndim == 3
    T = z.shape[1] if head_major else z.shape[0]
    _, M, mw = mk.shape
    hd = mw // MEM_HEADS
    qb = q_col // mw
    assert qb * mw == q_col and hd % LANES == 0
    per_b = S // tm
    if head_major:
        q_spec = pl.BlockSpec((mw // LANES, tm, LANES), lambda i: (qb, i, 0))
    else:
        q_spec = pl.BlockSpec((tm, mw), lambda i: (i, qb))
    kern = functools.partial(_memattn_kernel, heads=MEM_HEADS, hd=hd, head_major=head_major)
    return pl.pallas_call(
        kern,
        grid=(T // tm,),
        in_specs=[
            q_spec,
            pl.BlockSpec((None, None, 1, hd), lambda i: (layer, 0, 0, 0)),
            pl.BlockSpec((None, M, mw), lambda i: (i // per_b, 0, 0)),
            pl.BlockSpec((None, M, mw), lambda i: (i // per_b, 0, 0)),
        ],
        out_specs=pl.BlockSpec((tm, mw), lambda i: (i, 0)),
        out_shape=jax.ShapeDtypeStruct((T, mw), BF16),
        compiler_params=_params("parallel"),
        name="memattn",
    )(z, qk_gain, mk, mv)


def _outproj_kernel(ya_ref, ym_ref, wa_ref, wm_ref, x_ref, o_ref, ya_seq, ym_seq, *, chunk):
    @pl.when(pl.program_id(1) == 0)
    def _():
        _permute_chunks(ya_seq, ya_ref[...], chunk, inverse=True)
        _permute_chunks(ym_seq, ym_ref[...], chunk, inverse=True)

    o_ref[...] = x_ref[...] + _dot(ya_seq[...], wa_ref[...]) + _dot(ym_seq[...], wm_ref[...])


def _outproj_heads_kernel(ya_ref, ym_ref, wa_ref, wm_ref, x_ref, o_ref, cat_ref):
    @pl.when(pl.program_id(1) == 0)
    def _():
        for h in range(ya_ref.shape[0]):
            cat_ref[:, h * LANES:(h + 1) * LANES] = ya_ref[h]

    o_ref[...] = x_ref[...] + _dot(cat_ref[...], wa_ref[...]) + _dot(ym_ref[...], wm_ref[...])


def _outproj(x, ya, ym, w, tm, tn, chunk=None):
    T, D = x.shape
    head_major = ya.ndim == 3
    assert head_major == (chunk is None)
    ka = ya.shape[0] * LANES if head_major else ya.shape[1]
    km = ym.shape[1]
    mb = ka // km
    assert mb * km == ka
    if head_major:
        ya_spec = pl.BlockSpec((ka // LANES, tm, LANES), lambda i, j: (0, i, 0))
        kern, scratch = _outproj_heads_kernel, [pltpu.VMEM((tm, ka), BF16)]
    else:
        assert tm % chunk == 0
        ya_spec = pl.BlockSpec((tm, ka), lambda i, j: (i, 0))
        kern = functools.partial(_outproj_kernel, chunk=chunk)
        scratch = [pltpu.VMEM((tm, ka), BF16), pltpu.VMEM((tm, km), BF16)]
    return pl.pallas_call(
        kern,
        grid=(T // tm, D // tn),
        in_specs=[
            ya_spec,
            pl.BlockSpec((tm, km), lambda i, j: (i, 0)),
            pl.BlockSpec((None, ka, tn), lambda i, j: (0, 0, j)),
            pl.BlockSpec((None, km, tn), lambda i, j: (0, mb, j)),
            pl.BlockSpec((tm, tn), lambda i, j: (i, j)),
        ],
        out_specs=pl.BlockSpec((tm, tn), lambda i, j: (i, j)),
        out_shape=jax.ShapeDtypeStruct((T, D), F32),
        scratch_shapes=scratch,
        compiler_params=_params("parallel", "arbitrary"),
        name="outproj",
    )(ya, ym, w, w, x)


def _rec_kernel(*refs, reverse, final, ts, bw):
    if final:
        (prev_ref, main_ref, next_ref, cw_ref, cb_ref, gw_ref, gb_ref, lam_ref,
         gate_ref, hb_ref, o_ref, a_ref, u_ref, h_ref) = refs
    else:
        (prev_ref, main_ref, next_ref, cw_ref, cb_ref, gw_ref, gb_ref, lam_ref,
         o_ref, a_ref, u_ref, h_ref) = refs
    c = pl.program_id(1)
    nc = pl.num_programs(1)
    chunk = nc - 1 - c if reverse else c
    seg = ts // SUBLANES
    W = main_ref.shape[1]
    assert CONV_WIDTH == 4 and CONV_LEFT == 2

    @pl.when(c == 0)
    def _():
        h_ref[...] = jnp.zeros_like(h_ref)

    main = main_ref[...]
    prev = jnp.where(chunk > 0, prev_ref[...], 0.0)
    nxt = jnp.where(chunk < nc - 1, next_ref[...], 0.0)
    sub = lax.broadcasted_iota(jnp.int32, (SUBLANES, W), 0)

    def from_prev_segment(tile, fill):
        return jnp.where(sub == 0, fill, pltpu.roll(tile, 1, axis=0))

    def from_next_segment(tile, fill):
        return jnp.where(sub == SUBLANES - 1, fill, pltpu.roll(tile, SUBLANES - 1, axis=0))

    back1 = from_prev_segment(main[ts - SUBLANES:], prev[2 * SUBLANES - 1:])
    back2 = from_prev_segment(main[ts - 2 * SUBLANES:ts - SUBLANES], prev[SUBLANES - 1:SUBLANES])
    ahead1 = from_next_segment(main[:SUBLANES], nxt[0:1])
    taps = (jnp.concatenate([back2, back1, main[:ts - 2 * SUBLANES]], axis=0),
            jnp.concatenate([back1, main[:ts - SUBLANES]], axis=0),
            main,
            jnp.concatenate([main[SUBLANES:], ahead1], axis=0))
    xc = cb_ref[...]
    for k in range(CONV_WIDTH):
        xc = xc + cw_ref[k:k + 1, :] * taps[k]

    log_a_scale = (-0.5 * LRU_C) * jax.nn.softplus(-lam_ref[...])
    xh = 0.5 * xc
    xhb = xh.astype(BF16)
    gbh = 0.5 * gb_ref[...]
    for n in range(LRU_BLOCKS):
        sl = slice(n * bw, (n + 1) * bw)
        tr = jnp.tanh(_dot(xhb[:, sl], gw_ref[0, n]) + gbh[0:1, sl])
        ti = jnp.tanh(_dot(xhb[:, sl], gw_ref[1, n]) + gbh[1:2, sl])
        a = jnp.exp((tr + 1.0) * log_a_scale[:, sl])
        y = 1.0 - a * a
        a_ref[:, sl] = a
        u_ref[:, sl] = (y * lax.rsqrt(jnp.maximum(y, 1e-30))) * ((ti + 1.0) * xh[:, sl])

    def step(t, carry):
        h, prod = carry
        k = seg - 1 - t if reverse else t
        rows = pl.ds(pl.multiple_of(k * SUBLANES, SUBLANES), SUBLANES)
        a = a_ref[rows, :]
        h = a * h + u_ref[rows, :]
        prod = a * prod
        u_ref[rows, :] = h
        a_ref[rows, :] = prod
        return h, prod

    h_end, a_end = lax.fori_loop(0, seg, step, (jnp.zeros((SUBLANES, W), F32), jnp.ones((SUBLANES, W), F32)),
                                 unroll=2)

    carry = h_ref[...]
    h_in = jnp.zeros((SUBLANES, W), F32)
    for s in (range(SUBLANES - 1, -1, -1) if reverse else range(SUBLANES)):
        h_in = jnp.where(sub == s, carry, h_in)
        carry = h_end[s:s + 1] + a_end[s:s + 1] * carry
    h_ref[...] = carry

    hs = u_ref[...].reshape(seg, SUBLANES, W) + a_ref[...].reshape(seg, SUBLANES, W) * h_in[None]
    hs = hs.reshape(ts, W)
    if final:
        o_ref[...] = ((hs + hb_ref[...]) * jax.nn.gelu(gate_ref[...])).astype(o_ref.dtype)
    else:
        o_ref[...] = hs


def _rec_scan(z, conv_w, conv_b, gate_w, gate_b, lam, j, direction, h_other, S, ts):
    T = z.shape[0]
    W = conv_w.shape[2]
    bw = W // LRU_BLOCKS
    nc = S // ts
    reverse = direction == 1
    final = h_other is not None
    assert ts % (2 * SUBLANES) == 0

    def chunk_of(c):
        return nc - 1 - c if reverse else c

    def main_map(b, c):
        return (b * nc + chunk_of(c), 1)

    def prev_map(b, c):
        return (jnp.maximum((b * nc + chunk_of(c)) * (ts // (2 * SUBLANES)) - 1, 0), 1)

    def next_map(b, c):
        return (jnp.minimum((b * nc + chunk_of(c) + 1) * (ts // SUBLANES), T // SUBLANES - 1), 1)

    in_specs = [
        pl.BlockSpec((2 * SUBLANES, W), prev_map),
        pl.BlockSpec((ts, W), main_map),
        pl.BlockSpec((SUBLANES, W), next_map),
        pl.BlockSpec((None, CONV_WIDTH, W), lambda b, c: (j, 0, 0)),
        pl.BlockSpec((None, 1, W), lambda b, c: (j, 0, 0)),
        pl.BlockSpec((None, None, 2, LRU_BLOCKS, bw, bw), lambda b, c: (0, direction, 0, 0, 0, 0)),
        pl.BlockSpec((None, None, 2, W), lambda b, c: (j, direction, 0, 0)),
        pl.BlockSpec((None, None, 1, W), lambda b, c: (j, direction, 0, 0)),
    ]
    args = [z, z, z, conv_w, conv_b, gate_w, gate_b, lam]
    if final:
        in_specs += [
            pl.BlockSpec((ts, W), lambda b, c: (b * nc + chunk_of(c), 0)),
            pl.BlockSpec((ts, W), lambda b, c: (b * nc + chunk_of(c), 0)),
        ]
        args += [z, h_other]
    kern = functools.partial(_rec_kernel, reverse=reverse, final=final, ts=ts, bw=bw)
    return pl.pallas_call(
        kern,
        grid=(T // S, nc),
        in_specs=in_specs,
        out_specs=pl.BlockSpec((ts, W), lambda b, c: (b * nc + chunk_of(c), 0)),
        out_shape=jax.ShapeDtypeStruct((T, W), BF16 if final else F32),
        scratch_shapes=[
            pltpu.VMEM((ts, W), F32),
            pltpu.VMEM((ts, W), F32),
            pltpu.VMEM((1, W), F32),
        ],
        compiler_params=_params("parallel", "arbitrary"),
        name="rec_fwd" if final else "rec_bwd",
    )(*args)


ATT_TQ = 128
ATT_INTERLEAVE = 16


def _alibi_slopes():
    n = ATT_GROUPS * ATT_HEADS
    s = [2.0 ** (-8.0 * (i + 1.0) / n) for i in range(n)]
    return np.asarray(s, np.float32).reshape(ATT_GROUPS, ATT_HEADS)


def _rows(start, size, stride):
    return pl.ds(start, size) if stride == 1 else pl.ds(start, size, stride=stride)


def _attn_kernel(slopes_ref, *refs, span, halves):
    o_ref, acc_o, acc_l = refs[7 * ATT_GROUPS:]
    head = pl.program_id(1)
    blk = pl.program_id(2)
    nblk = pl.num_programs(2)
    tq, hd = ATT_TQ, ATT_HEAD_DIM
    half = halves[0]
    nk = tq + 2 * half
    row = lax.broadcasted_iota(jnp.int32, (tq, nk), 0)
    col = lax.broadcasted_iota(jnp.int32, (tq, nk), 1)
    rel = jnp.abs(col - half - row)
    in_band = rel <= half
    rel_f = rel.astype(F32)
    col1 = lax.broadcasted_iota(jnp.int32, (1, nk), 1)
    lo_mask = jnp.where((col1 < half) & (blk == 0), NEG_INF, 0.0)
    hi_mask = jnp.where((col1 >= nk - half) & (blk == nblk - 1), NEG_INF, 0.0)
    scale = hd ** -0.5

    def class_rows(before_ref, cur_ref, after_ref, d, r, lo, n):
        per_block = span // d
        pieces = []
        if d == 1:
            if lo < 0:
                pieces.append(before_ref[half + lo:half + min(lo + n, 0), :])
            if lo + n > 0 and lo < per_block:
                pieces.append(cur_ref[max(lo, 0):min(lo + n, per_block), :])
            if lo + n > per_block:
                pieces.append(after_ref[max(lo - per_block, 0):lo + n - per_block, :])
        else:
            per = ATT_CHUNK // d
            assert lo % per == 0 and n % per == 0 and half % per == 0
            for at in range(lo, lo + n, per):
                ref, base = ((before_ref, at + half) if at < 0 else
                             (after_ref, at - per_block) if at >= per_block else (cur_ref, at))
                first = (base // per) * ATT_CHUNK + r * per
                pieces.append(ref[first:first + per, :])
        return pieces[0] if len(pieces) == 1 else jnp.concatenate(pieces, axis=0)

    for g, (_, d) in enumerate(ATT_PATTERNS):
        assert halves[g] == half
        q_ref, kp_ref, kc_ref, kn_ref, vp_ref, vc_ref, vn_ref = refs[7 * g:7 * g + 7]
        nu = span // (tq * d)
        slope = slopes_ref[g, head]
        bias = jnp.where(in_band, -slope * (d * rel_f), NEG_INF)
        tiles = [(r, u) for r in range(d) for u in range(nu)]
        for first in range(0, len(tiles), ATT_INTERLEAVE):
            done = []
            for r, u in tiles[first:first + ATT_INTERLEAVE]:
                qrows = _rows(u * (tq * d) + r, tq, d)
                q = class_rows(None, q_ref, None, d, r, u * tq, tq)
                k = class_rows(kp_ref, kc_ref, kn_ref, d, r, u * tq - half, nk)
                v = class_rows(vp_ref, vc_ref, vn_ref, d, r, u * tq - half, nk)
                tile_bias = bias
                if u == 0:
                    tile_bias = tile_bias + lo_mask
                if u == nu - 1:
                    tile_bias = tile_bias + hi_mask
                s = _dot_nt(q, k) * scale + tile_bias
                m = jnp.max(s, axis=-1, keepdims=True)
                e = jnp.exp(s - m)
                den = jnp.sum(e, axis=-1, keepdims=True)
                o = _dot((e / den).astype(BF16), v)
                lse = jnp.broadcast_to(m + jnp.log(den), (tq, hd))
                prev = (acc_o[qrows, :], acc_l[qrows, :]) if g > 0 else None
                done.append((qrows, o, lse, prev))
            for qrows, o, lse, prev in done:
                if prev is not None:
                    o_acc, lse_acc = prev
                    top = jnp.maximum(lse_acc, lse)
                    w_acc = jnp.exp(lse_acc - top)
                    w_cur = jnp.exp(lse - top)
                    tot = w_acc + w_cur
                    o = (w_acc * o_acc + w_cur * o) / tot
                    lse = top + jnp.log(tot)
                acc_o[qrows, :] = o
                if g < ATT_GROUPS - 1:
                    acc_l[qrows, :] = lse

    o_ref[...] = acc_o[...].astype(o_ref.dtype)


def _attention(zh, B, S, span):
    T = zh.shape[1]
    nb = S // span
    halves = tuple(w // (2 * d) for w, d in ATT_PATTERNS)
    in_specs = [pl.BlockSpec(memory_space=pltpu.SMEM)]
    for g, (_, d) in enumerate(ATT_PATTERNS):
        halo = halves[g] * d
        assert span % (ATT_TQ * d) == 0 and span % halo == 0 and S % span == 0
        assert d == 1 or (halo % ATT_CHUNK == 0 and span % ATT_CHUNK == 0)
        ratio = span // halo

        def slab(p, g=g):
            return (3 * g + p) * ATT_HEADS

        def cur(p, slab=slab):
            return pl.BlockSpec((None, span, LANES), lambda b, h, i: (slab(p) + h, b * nb + i, 0))

        def before(p, slab=slab, halo=halo, ratio=ratio):
            return pl.BlockSpec((None, halo, LANES),
                                lambda b, h, i: (slab(p) + h, jnp.maximum((b * nb + i) * ratio - 1, 0), 0))

        def after(p, slab=slab, halo=halo, ratio=ratio):
            return pl.BlockSpec((None, halo, LANES),
                                lambda b, h, i: (slab(p) + h, jnp.minimum((b * nb + i + 1) * ratio, T // halo - 1), 0))

        in_specs += [cur(0), before(1), cur(1), after(1), before(2), cur(2), after(2)]
    kern = functools.partial(_attn_kernel, span=span, halves=halves)
    return pl.pallas_call(
        kern,
        grid=(B, ATT_HEADS, nb),
        in_specs=in_specs,
        out_specs=pl.BlockSpec((None, span, LANES), lambda b, h, i: (h, b * nb + i, 0)),
        out_shape=jax.ShapeDtypeStruct((ATT_HEADS, T, LANES), BF16),
        scratch_shapes=[pltpu.VMEM((span, LANES), F32), pltpu.VMEM((span, LANES), F32)],
        compiler_params=_params("parallel", "parallel", "arbitrary"),
        name="attn",
    )(jnp.asarray(_alibi_slopes()), *([zh] * (7 * ATT_GROUPS)))


def kernel(x, mem, ffn_norm, ffn_w_in, ffn_w_out, mix_norm, mem_norm, mem_w_kv, mem_qk_gain,
           rec_w_in, rec_conv_w, rec_conv_b, rec_gate_w, rec_gate_b, rec_lambda, rec_w_out,
           att_w_in, att_qk_gain, att_w_out):
    B, S, D = x.shape
    T = B * S
    depth = ffn_norm.shape[0]
    W = rec_conv_w.shape[2]

    tm_ffn = min(1024, S)
    tf = min(512, ffn_w_out.shape[2])
    tm_proj = min(1024, S)
    tn_proj = min(1024, D)
    ts_rec = min(256, S)
    span = ATT_TQ * max(d for _, d in ATT_PATTERNS)

    ffn_norm = ffn_norm[:, :, None, :]
    mix_norm = mix_norm[:, None, :]
    mem_norm = mem_norm[:, None, :]
    mem_qk_gain = mem_qk_gain[:, :, None, :]
    rec_conv_b = rec_conv_b[:, None, :]
    rec_gate_b = rec_gate_b.reshape(rec_gate_b.shape[0], 2, 2, W)
    rec_lambda = rec_lambda[:, :, None, :]
    att_qk_gain = att_qk_gain[:, :, :, None, :]
    gate_w_rows = rec_gate_w.reshape(rec_gate_w.shape[0], -1, rec_gate_w.shape[-1])

    def mixer_casts(layer):
        j = layer // 2
        if layer % 2 == 0:
            return [(rec_w_in, (j,)), (mem_w_kv, (layer,))], [(rec_w_out, (j,)), (gate_w_rows, (j,))]
        return [(att_w_in, (j,)), (mem_w_kv, (layer,))], [(att_w_out, (j,))]

    def cast_now(arr, lead):
        return arr[tuple(slice(i, i + 1) for i in lead)].astype(BF16)

    ffn_w = [cast_now(ffn_w_in, (0, 0)), cast_now(ffn_w_out, (0, 0))]
    mix_w = [cast_now(*c) for part in mixer_casts(0) for c in part]

    x = x.reshape(T, D)
    for layer in range(depth):
        more = layer + 1 < depth
        early, late = mixer_casts(layer + 1) if more else ([], [])
        x, done = _ffn(x, ffn_norm, *ffn_w, layer, 0, tm_ffn, tf,
                       [(ffn_w_in, (layer, 1)), (ffn_w_out, (layer, 1))] + early)
        ffn_w, next_mix_w = done[:2], list(done[2:])
        w_in, w_kv, w_out, *w_gate = mix_w
        mk, mv = _memkv(mem, mem_norm, w_kv, mem_qk_gain, layer)
        j = layer // 2
        if layer % 2 == 0:
            gate_w = w_gate[0].reshape((1,) + rec_gate_w.shape[1:])
            z = _inproj(x, mix_norm, w_in, layer, tm_proj, tn_proj, ts_rec)
            hb = _rec_scan(z, rec_conv_w, rec_conv_b, gate_w, rec_gate_b, rec_lambda, j, 1, None, S, ts_rec)
            ya = _rec_scan(z, rec_conv_w, rec_conv_b, gate_w, rec_gate_b, rec_lambda, j, 0, hb, S, ts_rec)
            ym = _memattn(z, 2 * W, mk, mv, mem_qk_gain, layer, S, tm_proj)
            x = _outproj(x, ya, ym, w_out, tm_proj, tn_proj, ts_rec)
        else:
            zh, qm = _inproj_heads(x, mix_norm, w_in, att_qk_gain, layer, j, tm_proj)
            ya = _attention(zh, B, S, span)
            ym = _memattn(qm, 0, mk, mv, mem_qk_gain, layer, S, tm_proj)
            x = _outproj(x, ya, ym, w_out, tm_proj, tn_proj)
        x, done = _ffn(x, ffn_norm, *ffn_w, layer, 1, tm_ffn, tf,
                       [(ffn_w_in, (layer + 1, 0)), (ffn_w_out, (layer + 1, 0))] + late if more else [])
        ffn_w, mix_w = done[:2], next_mix_w + list(done[2:])
    return x.reshape(B, S, D)
```

```python
import functools

import jax
import jax.numpy as jnp
import numpy as np
from jax import lax
from jax.experimental import pallas as pl
from jax.experimental.pallas import tpu as pltpu

F32 = jnp.float32
BF16 = jnp.bfloat16

EPS = 1e-6
NEG_INF = -1e30
LRU_C = 8.0
LRU_BLOCKS = 8
CONV_WIDTH = 4
CONV_LEFT = CONV_WIDTH // 2
ATT_PATTERNS = ((128, 1), (512, 4), (2048, 16))
ATT_GROUPS = len(ATT_PATTERNS)
ATT_HEADS = 8
ATT_HEAD_DIM = 128
ATT_WIDTH = ATT_HEADS * ATT_HEAD_DIM
MEM_HEADS = 4
ATT_CHUNK = 256
SUBLANES = 8
LANES = 128
BF16_ROWS = 16
VMEM_LIMIT_BYTES = 60 * 1024 * 1024


def _params(*semantics):
    return pltpu.CompilerParams(dimension_semantics=semantics, vmem_limit_bytes=VMEM_LIMIT_BYTES)


def _rms(x, g):
    return x * lax.rsqrt(jnp.mean(x * x, axis=-1, keepdims=True) + EPS) * g


def _dot(a, b):
    return jnp.dot(a, b, preferred_element_type=F32)


def _dot_nt(a, b):
    return lax.dot_general(a, b, (((1,), (1,)), ((), ())), preferred_element_type=F32)


def _ffn_kernel(*refs, n_casts):
    x_ref, g_ref, wg_ref, wu_ref, wo_ref = refs[:5]
    src_refs = refs[5:5 + n_casts]
    o_ref = refs[5 + n_casts]
    dst_refs = refs[6 + n_casts:6 + 2 * n_casts]
    xn_ref = refs[6 + 2 * n_casts]
    j = pl.program_id(1)

    @pl.when(j == 0)
    def _():
        xn_ref[...] = _rms(x_ref[...], g_ref[...]).astype(BF16)
        o_ref[...] = jnp.zeros_like(o_ref)

    xn = xn_ref[...]
    gate = _dot(xn, wg_ref[...])
    up = _dot(xn, wu_ref[...])
    h = (jax.nn.silu(gate) * up).astype(BF16)
    o_ref[...] += _dot(h, wo_ref[...])

    for src_ref, dst_ref in zip(src_refs, dst_refs):
        dst_ref[...] = src_ref[...].astype(BF16)

    @pl.when(j == pl.num_programs(1) - 1)
    def _():
        o_ref[...] = x_ref[...] + 0.5 * o_ref[...]


def _cast_rows(rows, steps):
    for rb in range(BF16_ROWS, rows + 1, BF16_ROWS):
        if rows % rb == 0 and rows // rb <= steps:
            return rb
    raise ValueError((rows, steps))


def _ffn(x, g, w_in, w_out, layer, k, tm, tf, casts=()):
    T, D = x.shape
    F = w_out.shape[2]
    nf = F // tf
    steps = (T // tm) * nf
    in_specs = [
        pl.BlockSpec((tm, D), lambda i, j: (i, 0)),
        pl.BlockSpec((None, None, 1, D), lambda i, j: (layer, k, 0, 0)),
        pl.BlockSpec((None, None, D, tf), lambda i, j: (0, 0, 0, j)),
        pl.BlockSpec((None, None, D, tf), lambda i, j: (0, 0, 0, j + nf)),
        pl.BlockSpec((None, None, tf, D), lambda i, j: (0, 0, j, 0)),
    ]
    out_specs = [pl.BlockSpec((tm, D), lambda i, j: (i, 0))]
    out_shape = [jax.ShapeDtypeStruct((T, D), F32)]
    for arr, lead in casts:
        rows, cols = arr.shape[-2:]
        rb = _cast_rows(rows, steps)
        nblk = rows // rb
        squeezed = (None,) * len(lead)

        def src_map(i, j, lead=lead, nblk=nblk):
            return (*lead, jnp.minimum(i * nf + j, nblk - 1), 0)

        def dst_map(i, j, lead=lead, nblk=nblk):
            return (*((0,) * len(lead)), jnp.minimum(i * nf + j, nblk - 1), 0)

        in_specs.append(pl.BlockSpec((*squeezed, rb, cols), src_map))
        out_specs.append(pl.BlockSpec((*squeezed, rb, cols), dst_map))
        out_shape.append(jax.ShapeDtypeStruct((1,) * len(lead) + (rows, cols), BF16))
    outs = pl.pallas_call(
        functools.partial(_ffn_kernel, n_casts=len(casts)),
        grid=(T // tm, nf),
        in_specs=in_specs,
        out_specs=out_specs,
        out_shape=out_shape,
        scratch_shapes=[pltpu.VMEM((tm, D), BF16)],
        compiler_params=_params("parallel", "arbitrary"),
        name="ffn",
    )(x, g, w_in, w_in, w_out, *(arr for arr, _ in casts))
    return outs[0], outs[1:]


def _segment_major_perm(chunk, inverse):
    seg = chunk // SUBLANES
    i = lax.broadcasted_iota(jnp.int32, (chunk, chunk), 0)
    j = lax.broadcasted_iota(jnp.int32, (chunk, chunk), 1)
    src = (i % seg) * SUBLANES + i // seg if inverse else (i % SUBLANES) * seg + i // SUBLANES
    return jnp.where(j == src, 1.0, 0.0).astype(BF16)


def _permute_chunks(dst_ref, src, chunk, inverse):
    perm = _segment_major_perm(chunk, inverse)
    for c in range(src.shape[0] // chunk):
        rows = slice(c * chunk, (c + 1) * chunk)
        dst_ref[rows, :] = _dot(perm, src[rows, :]).astype(BF16)


def _inproj_kernel(x_ref, g_ref, w_ref, o_ref, xn_ref, *, chunk):
    @pl.when(pl.program_id(1) == 0)
    def _():
        xn = _rms(x_ref[...], g_ref[...]).astype(BF16)
        _permute_chunks(xn_ref, xn, chunk, inverse=False)

    o_ref[...] = _dot(xn_ref[...], w_ref[...])


def _inproj(x, g, w, layer, tm, tn, chunk):
    T, D = x.shape
    N = w.shape[2]
    assert tm % chunk == 0
    return pl.pallas_call(
        functools.partial(_inproj_kernel, chunk=chunk),
        grid=(T // tm, N // tn),
        in_specs=[
            pl.BlockSpec((tm, D), lambda i, j: (i, 0)),
            pl.BlockSpec((None, 1, D), lambda i, j: (layer, 0, 0)),
            pl.BlockSpec((None, D, tn), lambda i, j: (0, 0, j)),
        ],
        out_specs=pl.BlockSpec((tm, tn), lambda i, j: (i, j)),
        out_shape=jax.ShapeDtypeStruct((T, N), F32),
        scratch_shapes=[pltpu.VMEM((tm, D), BF16)],
        compiler_params=_params("parallel", "arbitrary"),
        name="inproj",
    )(x, g, w)


def _class_major_perm(chunk, d):
    per = chunk // d
    i = lax.broadcasted_iota(jnp.int32, (chunk, chunk), 0)
    j = lax.broadcasted_iota(jnp.int32, (chunk, chunk), 1)
    return jnp.where(j == (i % per) * d + i // per, 1.0, 0.0).astype(BF16)


def _inproj_heads_kernel(x_ref, g_ref, w_ref, qkg_ref, o_ref, qm_ref, xn_ref):
    j = pl.program_id(1)

    @pl.when(j == 0)
    def _():
        xn = _rms(x_ref[...], g_ref[...]).astype(BF16)
        for grp, (_, d) in enumerate(ATT_PATTERNS):
            if d == 1:
                xn_ref[grp] = xn
            else:
                perm = _class_major_perm(ATT_CHUNK, d)
                for c in range(xn.shape[0] // ATT_CHUNK):
                    rows = slice(c * ATT_CHUNK, (c + 1) * ATT_CHUNK)
                    xn_ref[grp, rows, :] = _dot(perm, xn[rows, :]).astype(BF16)

    grp = jnp.where(j < 3 * ATT_GROUPS, j // 3, 0)
    is_qk = (j < 3 * ATT_GROUPS) & (j % 3 < 2)

    @pl.when(is_qk)
    def _():
        z = _dot(xn_ref[grp], w_ref[...])
        for h in range(o_ref.shape[0]):
            o_ref[h] = _rms(z[:, h * LANES:(h + 1) * LANES], qkg_ref[...]).astype(o_ref.dtype)

    @pl.when(jnp.logical_not(is_qk))
    def _():
        z = _dot(xn_ref[grp], w_ref[...])
        for h in range(o_ref.shape[0]):
            zh = z[:, h * LANES:(h + 1) * LANES]
            o_ref[h] = zh.astype(o_ref.dtype)
            qm_ref[h] = zh


def _inproj_heads(x, g, w, qk_gain, layer, att_layer, tm):
    T, D = x.shape
    N = w.shape[2]
    tn = ATT_WIDTH
    assert ATT_HEAD_DIM == LANES and N % tn == 0 and tm % ATT_CHUNK == 0
    assert all(ATT_CHUNK % d == 0 for _, d in ATT_PATTERNS)

    def gain_map(i, j):
        return (att_layer, jnp.minimum(j % 3, 1), jnp.minimum(j // 3, ATT_GROUPS - 1), 0, 0)

    return pl.pallas_call(
        _inproj_heads_kernel,
        grid=(T // tm, N // tn),
        in_specs=[
            pl.BlockSpec((tm, D), lambda i, j: (i, 0)),
            pl.BlockSpec((None, 1, D), lambda i, j: (layer, 0, 0)),
            pl.BlockSpec((None, D, tn), lambda i, j: (0, 0, j)),
            pl.BlockSpec((None, None, None, 1, LANES), gain_map),
        ],
        out_specs=[pl.BlockSpec((tn // LANES, tm, LANES), lambda i, j: (j, i, 0)),
                   pl.BlockSpec((tn // LANES, tm, LANES), lambda i, j: (0, i, 0))],
        out_shape=[jax.ShapeDtypeStruct((N // LANES, T, LANES), BF16),
                   jax.ShapeDtypeStruct((tn // LANES, T, LANES), F32)],
        scratch_shapes=[pltpu.VMEM((ATT_GROUPS, tm, D), BF16)],
        compiler_params=_params("parallel", "arbitrary"),
        name="inproj_heads",
    )(x, g, w, qk_gain)


def _memkv_kernel(mem_ref, g_ref, w_ref, kg_ref, k_ref, v_ref, *, heads, hd):
    mem_n = _rms(mem_ref[...], g_ref[...]).astype(BF16)
    kv = _dot(mem_n, w_ref[...])
    for h in range(heads):
        sl = slice(h * hd, (h + 1) * hd)
        k_ref[:, sl] = _rms(kv[:, sl], kg_ref[...]).astype(BF16)
    v_ref[...] = kv[:, heads * hd:].astype(BF16)


def _memkv(mem, g, w_kv, qk_gain, layer):
    B, M, D = mem.shape
    mw = w_kv.shape[2] // 2
    hd = mw // MEM_HEADS
    kern = functools.partial(_memkv_kernel, heads=MEM_HEADS, hd=hd)
    return pl.pallas_call(
        kern,
        grid=(B,),
        in_specs=[
            pl.BlockSpec((None, M, D), lambda b: (b, 0, 0)),
            pl.BlockSpec((None, 1, D), lambda b: (layer, 0, 0)),
            pl.BlockSpec((None, D, 2 * mw), lambda b: (0, 0, 0)),
            pl.BlockSpec((None, None, 1, hd), lambda b: (layer, 1, 0, 0)),
        ],
        out_specs=[pl.BlockSpec((None, M, mw), lambda b: (b, 0, 0))] * 2,
        out_shape=[jax.ShapeDtypeStruct((B, M, mw), BF16)] * 2,
        compiler_params=_params("parallel"),
        name="memkv",
    )(mem, g, w_kv, qk_gain)


def _memattn_kernel(q_ref, qg_ref, k_ref, v_ref, o_ref, *, heads, hd, head_major):
    scale = hd ** -0.5
    per = hd // LANES
    for h in range(heads):
        sl = slice(h * hd, (h + 1) * hd)
        if head_major:
            qh = jnp.concatenate([q_ref[h * per + c] for c in range(per)], axis=1)
        else:
            qh = q_ref[:, sl]
        q = _rms(qh, qg_ref[...]).astype(BF16)
        s = _dot_nt(q, k_ref[:, sl]) * scale
        e = jnp.exp(s - jnp.max(s, axis=-1, keepdims=True))
        p = e / jnp.sum(e, axis=-1, keepdims=True)
        o_ref[:, sl] = _dot(p.astype(BF16), v_ref[:, sl]).astype(o_ref.dtype)


def _memattn(z, q_col, mk, mv, qk_gain, layer, S, tm):
    head_major = z.ndim == 3
    T = z.shape[1] if head_major else z.shape[0]
    _, M, mw = mk.shape
    hd = mw // MEM_HEADS
    qb = q_col // mw
    assert qb * mw == q_col and hd % LANES == 0
    per_b = S // tm
    if head_major:
        q_spec = pl.BlockSpec((mw // LANES, tm, LANES), lambda i: (qb, i, 0))
    else:
        q_spec = pl.BlockSpec((tm, mw), lambda i: (i, qb))
    kern = functools.partial(_memattn_kernel, heads=MEM_HEADS, hd=hd, head_major=head_major)
    return pl.pallas_call(
        kern,
        grid=(T // tm,),
        in_specs=[
            q_spec,
            pl.BlockSpec((None, None, 1, hd), lambda i: (layer, 0, 0, 0)),
            pl.BlockSpec((None, M, mw), lambda i: (i // per_b, 0, 0)),
            pl.BlockSpec((None, M, mw), lambda i: (i // per_b, 0, 0)),
        ],
        out_specs=pl.BlockSpec((tm, mw), lambda i: (i, 0)),
        out_shape=jax.ShapeDtypeStruct((T, mw), BF16),
        compiler_params=_params("parallel"),
        name="memattn",
    )(z, qk_gain, mk, mv)


def _outproj_kernel(ya_ref, ym_ref, wa_ref, wm_ref, x_ref, o_ref, ya_seq, ym_seq, *, chunk):
    @pl.when(pl.program_id(1) == 0)
    def _():
        _permute_chunks(ya_seq, ya_ref[...], chunk, inverse=True)
        _permute_chunks(ym_seq, ym_ref[...], chunk, inverse=True)

    o_ref[...] = x_ref[...] + _dot(ya_seq[...], wa_ref[...]) + _dot(ym_seq[...], wm_ref[...])


def _outproj_heads_kernel(ya_ref, ym_ref, wa_ref, wm_ref, x_ref, o_ref, cat_ref):
    @pl.when(pl.program_id(1) == 0)
    def _():
        for h in range(ya_ref.shape[0]):
            cat_ref[:, h * LANES:(h + 1) * LANES] = ya_ref[h]

    o_ref[...] = x_ref[...] + _dot(cat_ref[...], wa_ref[...]) + _dot(ym_ref[...], wm_ref[...])


def _outproj(x, ya, ym, w, tm, tn, chunk=None):
    T, D = x.shape
    head_major = ya.ndim == 3
    assert head_major == (chunk is None)
    ka = ya.shape[0] * LANES if head_major else ya.shape[1]
    km = ym.shape[1]
    mb = ka // km
    assert mb * km == ka
    if head_major:
        ya_spec = pl.BlockSpec((ka // LANES, tm, LANES), lambda i, j: (0, i, 0))
        kern, scratch = _outproj_heads_kernel, [pltpu.VMEM((tm, ka), BF16)]
    else:
        assert tm % chunk == 0
        ya_spec = pl.BlockSpec((tm, ka), lambda i, j: (i, 0))
        kern = functools.partial(_outproj_kernel, chunk=chunk)
        scratch = [pltpu.VMEM((tm, ka), BF16), pltpu.VMEM((tm, km), BF16)]
    return pl.pallas_call(
        kern,
        grid=(T // tm, D // tn),
        in_specs=[
            ya_spec,
            pl.BlockSpec((tm, km), lambda i, j: (i, 0)),
            pl.BlockSpec((None, ka, tn), lambda i, j: (0, 0, j)),
            pl.BlockSpec((None, km, tn), lambda i, j: (0, mb, j)),
            pl.BlockSpec((tm, tn), lambda i, j: (i, j)),
        ],
        out_specs=pl.BlockSpec((tm, tn), lambda i, j: (i, j)),
        out_shape=jax.ShapeDtypeStruct((T, D), F32),
        scratch_shapes=scratch,
        compiler_params=_params("parallel", "arbitrary"),
        name="outproj",
    )(ya, ym, w, w, x)


def _rec_kernel(*refs, reverse, final, ts, bw):
    if final:
        (prev_ref, main_ref, next_ref, cw_ref, cb_ref, gw_ref, gb_ref, lam_ref,
         gate_ref, hb_ref, o_ref, a_ref, u_ref, h_ref) = refs
    else:
        (prev_ref, main_ref, next_ref, cw_ref, cb_ref, gw_ref, gb_ref, lam_ref,
         o_ref, a_ref, u_ref, h_ref) = refs
    c = pl.program_id(1)
    nc = pl.num_programs(1)
    chunk = nc - 1 - c if reverse else c
    seg = ts // SUBLANES
    W = main_ref.shape[1]
    assert CONV_WIDTH == 4 and CONV_LEFT == 2

    @pl.when(c == 0)
    def _():
        h_ref[...] = jnp.zeros_like(h_ref)

    main = main_ref[...]
    prev = jnp.where(chunk > 0, prev_ref[...], 0.0)
    nxt = jnp.where(chunk < nc - 1, next_ref[...], 0.0)
    sub = lax.broadcasted_iota(jnp.int32, (SUBLANES, W), 0)

    def from_prev_segment(tile, fill):
        return jnp.where(sub == 0, fill, pltpu.roll(tile, 1, axis=0))

    def from_next_segment(tile, fill):
        return jnp.where(sub == SUBLANES - 1, fill, pltpu.roll(tile, SUBLANES - 1, axis=0))

    back1 = from_prev_segment(main[ts - SUBLANES:], prev[2 * SUBLANES - 1:])
    back2 = from_prev_segment(main[ts - 2 * SUBLANES:ts - SUBLANES], prev[SUBLANES - 1:SUBLANES])
    ahead1 = from_next_segment(main[:SUBLANES], nxt[0:1])
    taps = (jnp.concatenate([back2, back1, main[:ts - 2 * SUBLANES]], axis=0),
            jnp.concatenate([back1, main[:ts - SUBLANES]], axis=0),
            main,
            jnp.concatenate([main[SUBLANES:], ahead1], axis=0))
    xc = cb_ref[...]
    for k in range(CONV_WIDTH):
        xc = xc + cw_ref[k:k + 1, :] * taps[k]

    log_a_scale = (-0.5 * LRU_C) * jax.nn.softplus(-lam_ref[...])
    xh = 0.5 * xc
    xhb = xh.astype(BF16)
    gbh = 0.5 * gb_ref[...]
    for n in range(LRU_BLOCKS):
        sl = slice(n * bw, (n + 1) * bw)
        tr = jnp.tanh(_dot(xhb[:, sl], gw_ref[0, n]) + gbh[0:1, sl])
        ti = jnp.tanh(_dot(xhb[:, sl], gw_ref[1, n]) + gbh[1:2, sl])
        a = jnp.exp((tr + 1.0) * log_a_scale[:, sl])
        y = 1.0 - a * a
        a_ref[:, sl] = a
        u_ref[:, sl] = (y * lax.rsqrt(jnp.maximum(y, 1e-30))) * ((ti + 1.0) * xh[:, sl])

    def step(t, carry):
        h, prod = carry
        k = seg - 1 - t if reverse else t
        rows = pl.ds(pl.multiple_of(k * SUBLANES, SUBLANES), SUBLANES)
        a = a_ref[rows, :]
        h = a * h + u_ref[rows, :]
        prod = a * prod
        u_ref[rows, :] = h
        a_ref[rows, :] = prod
        return h, prod

    h_end, a_end = lax.fori_loop(0, seg, step, (jnp.zeros((SUBLANES, W), F32), jnp.ones((SUBLANES, W), F32)),
                                 unroll=2)

    carry = h_ref[...]
    h_in = jnp.zeros((SUBLANES, W), F32)
    for s in (range(SUBLANES - 1, -1, -1) if reverse else range(SUBLANES)):
        h_in = jnp.where(sub == s, carry, h_in)
        carry = h_end[s:s + 1] + a_end[s:s + 1] * carry
    h_ref[...] = carry

    hs = u_ref[...].reshape(seg, SUBLANES, W) + a_ref[...].reshape(seg, SUBLANES, W) * h_in[None]
    hs = hs.reshape(ts, W)
    if final:
        o_ref[...] = ((hs + hb_ref[...]) * jax.nn.gelu(gate_ref[...])).astype(o_ref.dtype)
    else:
        o_ref[...] = hs


def _rec_scan(z, conv_w, conv_b, gate_w, gate_b, lam, j, direction, h_other, S, ts):
    T = z.shape[0]
    W = conv_w.shape[2]
    bw = W // LRU_BLOCKS
    nc = S // ts
    reverse = direction == 1
    final = h_other is not None
    assert ts % (2 * SUBLANES) == 0

    def chunk_of(c):
        return nc - 1 - c if reverse else c

    def main_map(b, c):
        return (b * nc + chunk_of(c), 1)

    def prev_map(b, c):
        return (jnp.maximum((b * nc + chunk_of(c)) * (ts // (2 * SUBLANES)) - 1, 0), 1)

    def next_map(b, c):
        return (jnp.minimum((b * nc + chunk_of(c) + 1) * (ts // SUBLANES), T // SUBLANES - 1), 1)

    in_specs = [
        pl.BlockSpec((2 * SUBLANES, W), prev_map),
        pl.BlockSpec((ts, W), main_map),
        pl.BlockSpec((SUBLANES, W), next_map),
        pl.BlockSpec((None, CONV_WIDTH, W), lambda b, c: (j, 0, 0)),
        pl.BlockSpec((None, 1, W), lambda b, c: (j, 0, 0)),
        pl.BlockSpec((None, None, 2, LRU_BLOCKS, bw, bw), lambda b, c: (0, direction, 0, 0, 0, 0)),
        pl.BlockSpec((None, None, 2, W), lambda b, c: (j, direction, 0, 0)),
        pl.BlockSpec((None, None, 1, W), lambda b, c: (j, direction, 0, 0)),
    ]
    args = [z, z, z, conv_w, conv_b, gate_w, gate_b, lam]
    if final:
        in_specs += [
            pl.BlockSpec((ts, W), lambda b, c: (b * nc + chunk_of(c), 0)),
            pl.BlockSpec((ts, W), lambda b, c: (b * nc + chunk_of(c), 0)),
        ]
        args += [z, h_other]
    kern = functools.partial(_rec_kernel, reverse=reverse, final=final, ts=ts, bw=bw)
    return pl.pallas_call(
        kern,
        grid=(T // S, nc),
        in_specs=in_specs,
        out_specs=pl.BlockSpec((ts, W), lambda b, c: (b * nc + chunk_of(c), 0)),
        out_shape=jax.ShapeDtypeStruct((T, W), BF16 if final else F32),
        scratch_shapes=[
            pltpu.VMEM((ts, W), F32),
            pltpu.VMEM((ts, W), F32),
            pltpu.VMEM((1, W), F32),
        ],
        compiler_params=_params("parallel", "arbitrary"),
        name="rec_fwd" if final else "rec_bwd",
    )(*args)


ATT_TQ = 128
ATT_INTERLEAVE = 16


def _alibi_slopes():
    n = ATT_GROUPS * ATT_HEADS
    s = [2.0 ** (-8.0 * (i + 1.0) / n) for i in range(n)]
    return np.asarray(s, np.float32).reshape(ATT_GROUPS, ATT_HEADS)


def _rows(start, size, stride):
    return pl.ds(start, size) if stride == 1 else pl.ds(start, size, stride=stride)


def _attn_kernel(slopes_ref, *refs, span, halves):
    o_ref, acc_o, acc_l = refs[7 * ATT_GROUPS:]
    head = pl.program_id(1)
    blk = pl.program_id(2)
    nblk = pl.num_programs(2)
    tq, hd = ATT_TQ, ATT_HEAD_DIM
    half = halves[0]
    nk = tq + 2 * half
    row = lax.broadcasted_iota(jnp.int32, (tq, nk), 0)
    col = lax.broadcasted_iota(jnp.int32, (tq, nk), 1)
    rel = jnp.abs(col - half - row)
    in_band = rel <= half
    rel_f = rel.astype(F32)
    col1 = lax.broadcasted_iota(jnp.int32, (1, nk), 1)
    lo_mask = jnp.where((col1 < half) & (blk == 0), NEG_INF, 0.0)
    hi_mask = jnp.where((col1 >= nk - half) & (blk == nblk - 1), NEG_INF, 0.0)
    scale = hd ** -0.5

    def class_rows(before_ref, cur_ref, after_ref, d, r, lo, n):
        per_block = span // d
        pieces = []
        if d == 1:
            if lo < 0:
                pieces.append(before_ref[half + lo:half + min(lo + n, 0), :])
            if lo + n > 0 and lo < per_block:
                pieces.append(cur_ref[max(lo, 0):min(lo + n, per_block), :])
            if lo + n > per_block:
                pieces.append(after_ref[max(lo - per_block, 0):lo + n - per_block, :])
        else:
            per = ATT_CHUNK // d
            assert lo % per == 0 and n % per == 0 and half % per == 0
            for at in range(lo, lo + n, per):
                ref, base = ((before_ref, at + half) if at < 0 else
                             (after_ref, at - per_block) if at >= per_block else (cur_ref, at))
                first = (base // per) * ATT_CHUNK + r * per
                pieces.append(ref[first:first + per, :])
        return pieces[0] if len(pieces) == 1 else jnp.concatenate(pieces, axis=0)

    for g, (_, d) in enumerate(ATT_PATTERNS):
        assert halves[g] == half
        q_ref, kp_ref, kc_ref, kn_ref, vp_ref, vc_ref, vn_ref = refs[7 * g:7 * g + 7]
        nu = span // (tq * d)
        slope = slopes_ref[g, head]
        bias = jnp.where(in_band, -slope * (d * rel_f), NEG_INF)
        tiles = [(r, u) for r in range(d) for u in range(nu)]
        for first in range(0, len(tiles), ATT_INTERLEAVE):
            done = []
            for r, u in tiles[first:first + ATT_INTERLEAVE]:
                qrows = _rows(u * (tq * d) + r, tq, d)
                q = class_rows(None, q_ref, None, d, r, u * tq, tq)
                k = class_rows(kp_ref, kc_ref, kn_ref, d, r, u * tq - half, nk)
                v = class_rows(vp_ref, vc_ref, vn_ref, d, r, u * tq - half, nk)
                tile_bias = bias
                if u == 0:
                    tile_bias = tile_bias + lo_mask
                if u == nu - 1:
                    tile_bias = tile_bias + hi_mask
                s = _dot_nt(q, k) * scale + tile_bias
                m = jnp.max(s, axis=-1, keepdims=True)
                e = jnp.exp(s - m)
                den = jnp.sum(e, axis=-1, keepdims=True)
                o = _dot((e / den).astype(BF16), v)
                lse = jnp.broadcast_to(m + jnp.log(den), (tq, hd))
                prev = (acc_o[qrows, :], acc_l[qrows, :]) if g > 0 else None
                done.append((qrows, o, lse, prev))
            for qrows, o, lse, prev in done:
                if prev is not None:
                    o_acc, lse_acc = prev
                    top = jnp.maximum(lse_acc, lse)
                    w_acc = jnp.exp(lse_acc - top)
                    w_cur = jnp.exp(lse - top)
                    tot = w_acc + w_cur
                    o = (w_acc * o_acc + w_cur * o) / tot
                    lse = top + jnp.log(tot)
                acc_o[qrows, :] = o
                if g < ATT_GROUPS - 1:
                    acc_l[qrows, :] = lse

    o_ref[...] = acc_o[...].astype(o_ref.dtype)


def _attention(zh, B, S, span):
    T = zh.shape[1]
    nb = S // span
    halves = tuple(w // (2 * d) for w, d in ATT_PATTERNS)
    in_specs = [pl.BlockSpec(memory_space=pltpu.SMEM)]
    for g, (_, d) in enumerate(ATT_PATTERNS):
        halo = halves[g] * d
        assert span % (ATT_TQ * d) == 0 and span % halo == 0 and S % span == 0
        assert d == 1 or (halo % ATT_CHUNK == 0 and span % ATT_CHUNK == 0)
        ratio = span // halo

        def slab(p, g=g):
            return (3 * g + p) * ATT_HEADS

        def cur(p, slab=slab):
            return pl.BlockSpec((None, span, LANES), lambda b, h, i: (slab(p) + h, b * nb + i, 0))

        def before(p, slab=slab, halo=halo, ratio=ratio):
            return pl.BlockSpec((None, halo, LANES),
                                lambda b, h, i: (slab(p) + h, jnp.maximum((b * nb + i) * ratio - 1, 0), 0))

        def after(p, slab=slab, halo=halo, ratio=ratio):
            return pl.BlockSpec((None, halo, LANES),
                                lambda b, h, i: (slab(p) + h, jnp.minimum((b * nb + i + 1) * ratio, T // halo - 1), 0))

        in_specs += [cur(0), before(1), cur(1), after(1), before(2), cur(2), after(2)]
    kern = functools.partial(_attn_kernel, span=span, halves=halves)
    return pl.pallas_call(
        kern,
        grid=(B, ATT_HEADS, nb),
        in_specs=in_specs,
        out_specs=pl.BlockSpec((None, span, LANES), lambda b, h, i: (h, b * nb + i, 0)),
        out_shape=jax.ShapeDtypeStruct((ATT_HEADS, T, LANES), BF16),
        scratch_shapes=[pltpu.VMEM((span, LANES), F32), pltpu.VMEM((span, LANES), F32)],
        compiler_params=_params("parallel", "parallel", "arbitrary"),
        name="attn",
    )(jnp.asarray(_alibi_slopes()), *([zh] * (7 * ATT_GROUPS)))


def kernel(x, mem, ffn_norm, ffn_w_in, ffn_w_out, mix_norm, mem_norm, mem_w_kv, mem_qk_gain,
           rec_w_in, rec_conv_w, rec_conv_b, rec_gate_w, rec_gate_b, rec_lambda, rec_w_out,
           att_w_in, att_qk_gain, att_w_out):
    B, S, D = x.shape
    T = B * S
    depth = ffn_norm.shape[0]
    W = rec_conv_w.shape[2]

    tm_ffn = min(1024, S)
    tf = min(512, ffn_w_out.shape[2])
    tm_proj = min(1024, S)
    tn_proj = min(1024, D)
    ts_rec = min(256, S)
    span = ATT_TQ * max(d for _, d in ATT_PATTERNS)

    ffn_norm = ffn_norm[:, :, None, :]
    mix_norm = mix_norm[:, None, :]
    mem_norm = mem_norm[:, None, :]
    mem_qk_gain = mem_qk_gain[:, :, None, :]
    rec_conv_b = rec_conv_b[:, None, :]
    rec_gate_b = rec_gate_b.reshape(rec_gate_b.shape[0], 2, 2, W)
    rec_lambda = rec_lambda[:, :, None, :]
    att_qk_gain = att_qk_gain[:, :, :, None, :]
    gate_w_rows = rec_gate_w.reshape(rec_gate_w.shape[0], -1, rec_gate_w.shape[-1])

    def mixer_casts(layer):
        j = layer // 2
        if layer % 2 == 0:
            return [(rec_w_in, (j,)), (mem_w_kv, (layer,))], [(rec_w_out, (j,)), (gate_w_rows, (j,))]
        return [(att_w_in, (j,)), (mem_w_kv, (layer,))], [(att_w_out, (j,))]

    def cast_now(arr, lead):
        return arr[tuple(slice(i, i + 1) for i in lead)].astype(BF16)

    ffn_w = [cast_now(ffn_w_in, (0, 0)), cast_now(ffn_w_out, (0, 0))]
    mix_w = [cast_now(*c) for part in mixer_casts(0) for c in part]

    x = x.reshape(T, D)
    for layer in range(depth):
        more = layer + 1 < depth
        early, late = mixer_casts(layer + 1) if more else ([], [])
        x, done = _ffn(x, ffn_norm, *ffn_w, layer, 0, tm_ffn, tf,
                       [(ffn_w_in, (layer, 1)), (ffn_w_out, (layer, 1))] + early)
        ffn_w, next_mix_w = done[:2], list(done[2:])
        w_in, w_kv, w_out, *w_gate = mix_w
        mk, mv = _memkv(mem, mem_norm, w_kv, mem_qk_gain, layer)
        j = layer // 2
        if layer % 2 == 0:
            gate_w = w_gate[0].reshape((1,) + rec_gate_w.shape[1:])
            z = _inproj(x, mix_norm, w_in, layer, tm_proj, tn_proj, ts_rec)
            hb = _rec_scan(z, rec_conv_w, rec_conv_b, gate_w, rec_gate_b, rec_lambda, j, 1, None, S, ts_rec)
            ya = _rec_scan(z, rec_conv_w, rec_conv_b, gate_w, rec_gate_b, rec_lambda, j, 0, hb, S, ts_rec)
            ym = _memattn(z, 2 * W, mk, mv, mem_qk_gain, layer, S, tm_proj)
            x = _outproj(x, ya, ym, w_out, tm_proj, tn_proj, ts_rec)
        else:
            zh, qm = _inproj_heads(x, mix_norm, w_in, att_qk_gain, layer, j, tm_proj)
            ya = _attention(zh, B, S, span)
            ym = _memattn(qm, 0, mk, mv, mem_qk_gain, layer, S, tm_proj)
            x = _outproj(x, ya, ym, w_out, tm_proj, tn_proj)
        x, done = _ffn(x, ffn_norm, *ffn_w, layer, 1, tm_ffn, tf,
                       [(ffn_w_in, (layer + 1, 0)), (ffn_w_out, (layer + 1, 0))] + late if more else [])
        ffn_w, mix_w = done[:2], next_mix_w + list(done[2:])
    return x.reshape(B, S, D)
```
